```python
import math
import jax, jax.numpy as jnp
from jax import lax
import numpy as np

D_MODEL = 1024
BATCH = 16
SEQ = 4096
DEPTH = 1
DEC_BATCH = 8
DEC_SEQ = 64
PAST_LEN = 1024

CHUNK = 64
QBLOCK = 128
ROPE_THETA = 500000.0
RMS_EPS = 1e-6
DA_HEADS = 4
DA_QK_DIM = 64
DA_V_DIM = 2 * DA_QK_DIM
DA_WIDTH = DA_HEADS * DA_V_DIM
DA_ROT = DA_QK_DIM // 4
DA_QK_COLS = DA_HEADS * 2 * DA_QK_DIM
MLA_HEADS = 8
MLA_Q_LORA = 384
MLA_KV_LORA = 256
MLA_NOPE = 64
MLA_ROPE = 32
MLA_V_DIM = 64
MLA_WIDTH = MLA_HEADS * MLA_V_DIM
MLA_UQ_COLS = MLA_HEADS * (MLA_NOPE + MLA_ROPE)
IN_SIZES = (DA_QK_COLS, DA_QK_COLS, DA_WIDTH, DA_WIDTH, MLA_Q_LORA, MLA_KV_LORA, MLA_ROPE, MLA_WIDTH, 2 * D_MODEL)
IN_COLS = sum(IN_SIZES)

kernel_name = "hybrid_diffattn_mla_streaming_step"


def rms_norm(x, g):
    xf = x.astype(jnp.float32)
    y = xf * lax.rsqrt(jnp.mean(xf * xf, axis=-1, keepdims=True) + RMS_EPS)
    return (y * g.astype(jnp.float32)).astype(x.dtype)


def rope(x, pos, rot):
    half = rot // 2
    inv = jnp.float32(ROPE_THETA) ** (-jnp.arange(half, dtype=jnp.float32) * 2.0 / rot)
    ang = pos.astype(jnp.float32)[:, None] * inv
    ang = ang.reshape((ang.shape[0],) + (1,) * (x.ndim - 3) + (half,))
    cos = jnp.cos(ang).astype(x.dtype)
    sin = jnp.sin(ang).astype(x.dtype)
    x1 = x[..., :half]
    x2 = x[..., half:rot]
    return jnp.concatenate([x1 * cos - x2 * sin, x2 * cos + x1 * sin, x[..., rot:]], axis=-1)


def chunk_mask(q_pos, k_pos):
    return (k_pos // CHUNK)[None, :] <= (q_pos // CHUNK)[:, None]


def masked_softmax(s, mask):
    return jax.nn.softmax(jnp.where(mask, s.astype(jnp.float32), -jnp.inf), axis=-1)


def over_query_blocks(fn, qs, q_pos):
    S = q_pos.shape[0]
    if S <= QBLOCK:
        return fn(qs, q_pos)
    nb = S // QBLOCK

    def body(i):
        start = i * QBLOCK
        qb = tuple(lax.dynamic_slice_in_dim(q, start, QBLOCK, axis=1) for q in qs)
        pb = lax.dynamic_slice_in_dim(q_pos, start, QBLOCK)
        return fn(qb, pb)

    out = lax.map(body, jnp.arange(nb))
    out = jnp.moveaxis(out, 0, 1)
    return out.reshape((out.shape[0], S) + out.shape[3:])


def diff_attention(q, k, v, q_pos, k_pos, lam):
    scale = DA_QK_DIM ** -0.5

    def fn(qs, qp):
        (qb,) = qs
        s = jnp.einsum('bqhcd,bkhcd->bchqk', qb, k) * scale
        p = masked_softmax(s, chunk_mask(qp, k_pos))
        a = (p[:, 0] - lam * p[:, 1]).astype(v.dtype)
        return jnp.einsum('bhqk,bkhd->bqhd', a, v)

    return over_query_blocks(fn, (q,), q_pos)


def mla_attention(q, k, v, q_pos, k_pos):
    scale = (MLA_NOPE + MLA_ROPE) ** -0.5

    def fn(qs, qp):
        (qb,) = qs
        s = jnp.einsum('bqhd,bkhd->bhqk', qb, k) * scale
        p = masked_softmax(s, chunk_mask(qp, k_pos)).astype(v.dtype)
        return jnp.einsum('bhqk,bkhd->bqhd', p, v)

    return over_query_blocks(fn, (q,), q_pos)


def layer(x, past, lam_init, norm_g, w_in, gate_b, da_lambda, da_head_norm_g, mla_q_norm_g,
          mla_kv_norm_g, mla_w_uq, mla_w_uk, mla_w_uv, w_branch_a, w_branch_b, w_out):
    B, S, _ = x.shape
    past_len = 0 if past is None else past[0].shape[1]
    pos = past_len + jnp.arange(S, dtype=jnp.int32)
    k_pos = jnp.arange(past_len + S, dtype=jnp.int32)

    h = rms_norm(x, norm_g)
    proj = h @ w_in
    idx = np.cumsum(IN_SIZES)[:-1].tolist()
    da_q, da_k, da_v, da_z, cq, ckv, krope, mla_z, gates = jnp.split(proj, idx, axis=-1)

    q = rope(da_q.reshape(B, S, DA_HEADS, 2, DA_QK_DIM), pos, DA_ROT)
    k = rope(da_k.reshape(B, S, DA_HEADS, 2, DA_QK_DIM), pos, DA_ROT)
    v = da_v.reshape(B, S, DA_HEADS, DA_V_DIM)
    new_k = k.reshape(B, S, DA_HEADS, 2 * DA_QK_DIM)
    new_v = v
    c_kv = rms_norm(ckv, mla_kv_norm_g)
    k_r = rope(krope, pos, MLA_ROPE)

    if past is None:
        k_all, v_all, c_all, kr_all = k, v, c_kv, k_r
    else:
        pk, pv, pc, pr = past
        k_all = jnp.concatenate([pk.reshape(B, past_len, DA_HEADS, 2, DA_QK_DIM), k], axis=1)
        v_all = jnp.concatenate([pv, v], axis=1)
        c_all = jnp.concatenate([pc, c_kv], axis=1)
        kr_all = jnp.concatenate([pr, k_r], axis=1)
    T = past_len + S

    lf = da_lambda.astype(jnp.float32)
    lam = jnp.exp(jnp.sum(lf[0] * lf[1])) - jnp.exp(jnp.sum(lf[2] * lf[3])) + lam_init
    o_a = diff_attention(q, k_all, v_all, pos, k_pos, lam)
    o_a = (rms_norm(o_a, da_head_norm_g) * (1.0 - lam_init)).reshape(B, S, DA_WIDTH)

    qf = (rms_norm(cq, mla_q_norm_g) @ mla_w_uq).reshape(B, S, MLA_HEADS, MLA_NOPE + MLA_ROPE)
    q_m = jnp.concatenate([qf[..., :MLA_NOPE], rope(qf[..., MLA_NOPE:], pos, MLA_ROPE)], axis=-1)
    k_nope = (c_all @ mla_w_uk).reshape(B, T, MLA_HEADS, MLA_NOPE)
    v_m = (c_all @ mla_w_uv).reshape(B, T, MLA_HEADS, MLA_V_DIM)
    k_m = jnp.concatenate([k_nope, jnp.broadcast_to(kr_all[:, :, None, :], (B, T, MLA_HEADS, MLA_ROPE))], axis=-1)
    o_b = mla_attention(q_m, k_m, v_m, pos, k_pos).reshape(B, S, MLA_WIDTH)

    y_a = (o_a * jax.nn.silu(da_z)) @ w_branch_a
    y_b = (o_b * jax.nn.silu(mla_z)) @ w_branch_b
    g = jax.nn.sigmoid(gates + gate_b)
    m = g[..., :D_MODEL] * y_a + g[..., D_MODEL:] * y_b
    out = x + m @ w_out
    return out, (new_k, new_v, c_kv, k_r)


def setup_inputs(seed: int = 0) -> dict:
    key = jax.random.key(seed)
    ks = jax.random.split(key, 24)
    nrm = lambda k, shape, s: jax.random.normal(k, shape, jnp.float32) * s
    return {
        'x_prompt': nrm(ks[0], (BATCH, SEQ, D_MODEL), 1.0),
        'x_sample': nrm(ks[1], (DEC_BATCH, DEC_SEQ, D_MODEL), 1.0),
        'cache_da_k': nrm(ks[2], (DEPTH, DEC_BATCH, PAST_LEN, DA_HEADS, 2 * DA_QK_DIM), 1.0),
        'cache_da_v': nrm(ks[3], (DEPTH, DEC_BATCH, PAST_LEN, DA_HEADS, DA_V_DIM), 1.0),
        'cache_mla_latent': nrm(ks[4], (DEPTH, DEC_BATCH, PAST_LEN, MLA_KV_LORA), 1.0),
        'cache_mla_krope': nrm(ks[5], (DEPTH, DEC_BATCH, PAST_LEN, MLA_ROPE), 1.0),
        'norm_g': 1.0 + nrm(ks[6], (DEPTH, D_MODEL), 0.01),
        'w_in': nrm(ks[7], (DEPTH, D_MODEL, IN_COLS), D_MODEL ** -0.5),
        'gate_b': nrm(ks[8], (DEPTH, 2 * D_MODEL), 0.01),
        'da_lambda': nrm(ks[9], (DEPTH, 4, DA_QK_DIM), 0.1),
        'da_head_norm_g': 1.0 + nrm(ks[10], (DEPTH, DA_V_DIM), 0.01),
        'mla_q_norm_g': 1.0 + nrm(ks[11], (DEPTH, MLA_Q_LORA), 0.01),
        'mla_kv_norm_g': 1.0 + nrm(ks[12], (DEPTH, MLA_KV_LORA), 0.01),
        'mla_w_uq': nrm(ks[13], (DEPTH, MLA_Q_LORA, MLA_UQ_COLS), MLA_Q_LORA ** -0.5),
        'mla_w_uk': nrm(ks[14], (DEPTH, MLA_KV_LORA, MLA_HEADS * MLA_NOPE), MLA_KV_LORA ** -0.5),
        'mla_w_uv': nrm(ks[15], (DEPTH, MLA_KV_LORA, MLA_WIDTH), MLA_KV_LORA ** -0.5),
        'w_branch_a': nrm(ks[16], (DEPTH, DA_WIDTH, D_MODEL), DA_WIDTH ** -0.5),
        'w_branch_b': nrm(ks[17], (DEPTH, MLA_WIDTH, D_MODEL), MLA_WIDTH ** -0.5),
        'w_out': nrm(ks[18], (DEPTH, D_MODEL, D_MODEL), D_MODEL ** -0.5),
        'final_norm_g': 1.0 + nrm(ks[19], (D_MODEL,), 0.01),
    }


def reference(x_prompt, x_sample, cache_da_k, cache_da_v, cache_mla_latent, cache_mla_krope,
              norm_g, w_in, gate_b, da_lambda, da_head_norm_g, mla_q_norm_g, mla_kv_norm_g,
              mla_w_uq, mla_w_uk, mla_w_uv, w_branch_a, w_branch_b, w_out, final_norm_g):
    hp, hs = x_prompt, x_sample
    rows_p = ([], [], [], [])
    rows_s = ([], [], [], [])
    for l in range(DEPTH):
        lam_init = 0.8 - 0.6 * math.exp(-0.3 * l)
        w = (norm_g[l], w_in[l], gate_b[l], da_lambda[l], da_head_norm_g[l], mla_q_norm_g[l],
             mla_kv_norm_g[l], mla_w_uq[l], mla_w_uk[l], mla_w_uv[l], w_branch_a[l], w_branch_b[l], w_out[l])
        hp, new_p = layer(hp, None, lam_init, *w)
        past = (cache_da_k[l], cache_da_v[l], cache_mla_latent[l], cache_mla_krope[l])
        hs, new_s = layer(hs, past, lam_init, *w)
        for acc, r in zip(rows_p, new_p):
            acc.append(r)
        for acc, r in zip(rows_s, new_s):
            acc.append(r)
    y_prompt = rms_norm(hp, final_norm_g)
    y_sample = rms_norm(hs, final_norm_g)
    new_da_k_p = jnp.stack(rows_p[0], 0)
    new_da_v_p = jnp.stack(rows_p[1], 0)
    new_lat_p = jnp.stack(rows_p[2], 0)
    new_krope_p = jnp.stack(rows_p[3], 0)
    new_da_k_s = jnp.stack(rows_s[0], 0)
    new_da_v_s = jnp.stack(rows_s[1], 0)
    new_lat_s = jnp.stack(rows_s[2], 0)
    new_krope_s = jnp.stack(rows_s[3], 0)
    return (y_prompt, y_sample, new_da_k_p, new_da_v_p, new_lat_p, new_krope_p,
            new_da_k_s, new_da_v_s, new_lat_s, new_krope_s)
```

```python
import functools
import math

import jax
import jax.numpy as jnp
import numpy as np
from jax import lax
from jax.experimental import pallas as pl
from jax.experimental.pallas import tpu as pltpu

CHUNK = 64
ROPE_THETA = 500000.0
RMS_EPS = 1e-6
DA_HEADS = 4
DA_QK_DIM = 64
DA_V_DIM = 2 * DA_QK_DIM
DA_ROT = DA_QK_DIM // 4
DA_COLS = DA_HEADS * DA_V_DIM
MLA_HEADS = 8
MLA_Q_LORA = 384
MLA_KV_LORA = 256
MLA_NOPE = 64
MLA_ROPE = 32
MLA_V_DIM = 64
MLA_WIDTH = MLA_HEADS * MLA_V_DIM

LANES = 128
MLA_QK_COLS = MLA_HEADS * LANES
LOG2E = math.log2(math.e)
NEG_BIG = -1e30
VMEM_LIMIT_BYTES = 56 * 1024 * 1024

F32 = jnp.float32
BF16 = jnp.bfloat16


def _rms(x, g):
    return x * lax.rsqrt(jnp.mean(x * x, axis=-1, keepdims=True) + RMS_EPS) * g


def _rope128(x, tab_ref, shift):
    return (x * tab_ref[0] + pltpu.roll(x, LANES - shift, 1) * tab_ref[1]
            + pltpu.roll(x, shift, 1) * tab_ref[2])


def _rope_tables(pos, rot, lane_of):
    half = rot // 2
    inv = jnp.float32(ROPE_THETA) ** (-jnp.arange(half, dtype=F32) * 2.0 / rot)
    ang = pos.astype(F32)[:, None] * inv
    cos, sin = jnp.cos(ang), jnp.sin(ang)
    idx = np.array([lane_of(j) for j in range(LANES)])
    lo = (idx >= 0) & (idx < half)
    hi = idx >= half
    src = np.where(idx >= 0, idx % half, 0)
    c = jnp.where(jnp.asarray(lo | hi), cos[:, src], 1.0)
    sa = jnp.where(jnp.asarray(lo), -sin[:, src], 0.0)
    sb = jnp.where(jnp.asarray(hi), sin[:, src], 0.0)
    return jnp.stack([c, sa, sb]).astype(F32)


def _da_lane(j):
    c = j % DA_QK_DIM
    return c if c < DA_ROT else -1


def _mla_lane(j):
    c = j - MLA_NOPE
    return c if 0 <= c < MLA_ROPE else -1


def _inproj_kernel(x_ref, ng_ref, wm_ref, qg_ref, kvg_ref, wuq_ref, tda_ref, tml_ref,
                   qT_ref, k_ref, v_ref, lat_ref, kr_ref, qmT_ref, *, q_scale, qm_scale):
    h = _rms(x_ref[...], ng_ref[...]).astype(BF16)
    proj = jnp.dot(h, wm_ref[...], preferred_element_type=F32)
    o_k, o_v = DA_COLS, 2 * DA_COLS
    o_cq = 3 * DA_COLS
    o_ckv = o_cq + MLA_Q_LORA
    o_kr = o_ckv + MLA_KV_LORA
    for i in range(DA_HEADS):
        lo, hi = i * LANES, (i + 1) * LANES
        q = _rope128(proj[:, lo:hi], tda_ref, DA_ROT // 2) * q_scale
        qT_ref[0, lo:hi, :] = q.T.astype(BF16)
        k_ref[:, lo:hi] = _rope128(proj[:, o_k + lo:o_k + hi], tda_ref, DA_ROT // 2)
    v_ref[...] = proj[:, o_v:o_v + DA_COLS]
    lat_ref[...] = _rms(proj[:, o_ckv:o_ckv + MLA_KV_LORA], kvg_ref[...])
    kr_ref[...] = _rope128(proj[:, o_kr:o_kr + LANES], tml_ref, MLA_ROPE // 2)
    qn = _rms(proj[:, o_cq:o_cq + MLA_Q_LORA], qg_ref[...]).astype(BF16)
    qm = jnp.dot(qn, wuq_ref[...], preferred_element_type=F32)
    for i in range(MLA_HEADS):
        lo, hi = i * LANES, (i + 1) * LANES
        q = _rope128(qm[:, lo:hi], tml_ref, MLA_ROPE // 2) * qm_scale
        qmT_ref[0, lo:hi, :] = q.T.astype(BF16)


def _inproj(x, norm_g, w_main, qg, kvg, wuq, tda, tml, *, tm):
    B, S, D = x.shape
    n = B * S
    nst = S // tm
    wcols = w_main.shape[1]
    const = lambda i: (0, 0)
    tok = lambda i: (i, 0)
    feat_t = lambda i: (i // nst, 0, i % nst)
    tab = lambda i: (0, i % nst, 0)
    return pl.pallas_call(
        functools.partial(_inproj_kernel, q_scale=DA_QK_DIM ** -0.5 * LOG2E,
                          qm_scale=(MLA_NOPE + MLA_ROPE) ** -0.5 * LOG2E),
        grid=(n // tm,),
        in_specs=[
            pl.BlockSpec((tm, D), tok),
            pl.BlockSpec((1, D), const),
            pl.BlockSpec((D, wcols), const),
            pl.BlockSpec((1, MLA_Q_LORA), const),
            pl.BlockSpec((1, MLA_KV_LORA), const),
            pl.BlockSpec((MLA_Q_LORA, MLA_QK_COLS), const),
            pl.BlockSpec((3, tm, LANES), tab),
            pl.BlockSpec((3, tm, LANES), tab),
        ],
        out_specs=[
            pl.BlockSpec((1, DA_COLS, tm), feat_t),
            pl.BlockSpec((tm, DA_COLS), tok),
            pl.BlockSpec((tm, DA_COLS), tok),
            pl.BlockSpec((tm, MLA_KV_LORA), tok),
            pl.BlockSpec((tm, LANES), tok),
            pl.BlockSpec((1, MLA_QK_COLS, tm), feat_t),
        ],
        out_shape=[
            jax.ShapeDtypeStruct((B, DA_COLS, S), BF16),
            jax.ShapeDtypeStruct((n, DA_COLS), F32),
            jax.ShapeDtypeStruct((n, DA_COLS), F32),
            jax.ShapeDtypeStruct((n, MLA_KV_LORA), F32),
            jax.ShapeDtypeStruct((n, LANES), F32),
            jax.ShapeDtypeStruct((B, MLA_QK_COLS, S), BF16),
        ],
        compiler_params=pltpu.CompilerParams(
            dimension_semantics=("arbitrary",), vmem_limit_bytes=VMEM_LIMIT_BYTES),
        name="inproj",
    )(x.reshape(n, D), norm_g, w_main, qg, kvg, wuq, tda, tml)


def _expand_kernel(c_ref, kr_ref, wuk_ref, wuv_ref, km_ref, vT_ref):
    c = c_ref[0].astype(BF16)
    kn = jnp.dot(c, wuk_ref[...], preferred_element_type=F32)
    kr = kr_ref[0]
    for i in range(MLA_HEADS):
        lo, hi = i * LANES, (i + 1) * LANES
        km_ref[0, :, lo:hi] = (kn[:, lo:hi] + kr).astype(BF16)
    vm = jnp.dot(c, wuv_ref[...], preferred_element_type=F32)
    for i in range(MLA_WIDTH // LANES):
        lo, hi = i * LANES, (i + 1) * LANES
        vT_ref[0, 0, lo:hi, :] = vm[:, lo:hi].T.astype(BF16)


def _expand(c, kr, wuk, wuv, *, tk):
    B, T, _ = c.shape
    nkv = T // tk
    const = lambda b, j: (0, 0)
    tile = lambda b, j: (b, j, 0)
    return pl.pallas_call(
        _expand_kernel,
        grid=(B, nkv),
        in_specs=[
            pl.BlockSpec((1, tk, MLA_KV_LORA), tile),
            pl.BlockSpec((1, tk, LANES), tile),
            pl.BlockSpec((MLA_KV_LORA, MLA_QK_COLS), const),
            pl.BlockSpec((MLA_KV_LORA, MLA_WIDTH), const),
        ],
        out_specs=[
            pl.BlockSpec((1, tk, MLA_QK_COLS), tile),
            pl.BlockSpec((1, 1, MLA_WIDTH, tk), lambda b, j: (b, j, 0, 0)),
        ],
        out_shape=[
            jax.ShapeDtypeStruct((B, T, MLA_QK_COLS), BF16),
            jax.ShapeDtypeStruct((B, nkv, MLA_WIDTH, tk), BF16),
        ],
        compiler_params=pltpu.CompilerParams(
            dimension_semantics=("arbitrary", "arbitrary"), vmem_limit_bytes=VMEM_LIMIT_BYTES),
        name="kv_expand",
    )(c, kr, wuk, wuv)


def _tile_counts(qi, *, tq, tk, nkv, q_off, kv_len):
    n_tiles = jnp.minimum(nkv, (q_off + (qi + 1) * tq - 1) // tk + 1)
    n_full = jnp.minimum(jnp.minimum((q_off + qi * tq + CHUNK) // tk, kv_len // tk), n_tiles)
    return n_tiles, n_full


def _flash_tiles(n_tiles, n_full, load_k, load_vT, load_q, rows, m_ref, l_ref, acc_ref,
                 *, qi, tq, tk, q_off, kv_len):
    ncols = m_ref.shape[1]
    m_ref[...] = jnp.full(m_ref.shape, NEG_BIG, F32)
    l_ref[...] = jnp.zeros(l_ref.shape, F32)
    acc_ref[rows, :] = jnp.zeros((rows.stop - rows.start, ncols), F32)

    def step(j, masked):
        s = jnp.dot(load_k(j), load_q(), preferred_element_type=F32)
        if masked:
            kpos = j * tk + lax.broadcasted_iota(jnp.int32, (tk, 1), 0)
            col = lax.broadcasted_iota(jnp.int32, (1, ncols), 1)
            qpos = q_off + qi * tq + jnp.where(col >= tq, col - tq, col)
            ok = (lax.shift_right_logical(kpos, 6) <= lax.shift_right_logical(qpos, 6)) & (kpos < kv_len)
            s = jnp.where(ok, s, NEG_BIG)
        m_prev = m_ref[...]
        m_new = jnp.maximum(m_prev, jnp.max(s, axis=0, keepdims=True))
        alpha = jnp.exp2(m_prev - m_new)
        p = jnp.exp2(s - m_new)
        l_ref[...] = alpha * l_ref[...] + jnp.sum(p, axis=0, keepdims=True)
        pv = jnp.dot(load_vT(j), p.astype(BF16), preferred_element_type=F32)
        acc_ref[rows, :] = alpha * acc_ref[rows, :] + pv
        m_ref[...] = m_new

    def full_body(j, carry):
        step(j, False)
        return carry

    def masked_body(j, carry):
        step(j, True)
        return carry

    lax.fori_loop(0, n_full, full_body, 0)
    lax.fori_loop(n_full, n_tiles, masked_body, 0)


def _da_attn_kernel(lam_ref, hg_ref, qT_ref, k_ref, v_ref, o_ref,
                    kb_ref, vT_ref, qs_ref, m_ref, l_ref, acc_ref,
                    *, tq, tk, nkv, q_off, kv_len, lam_init):
    qi = pl.program_id(2)

    @pl.when(qi == 0)
    def _():
        for c in range(nkv):
            kb_ref[c] = k_ref[0, c * tk:(c + 1) * tk, :].astype(BF16)
            vT_ref[c] = v_ref[0, c * tk:(c + 1) * tk, :].T.astype(BF16)

    qT = qT_ref[0]
    row = lax.broadcasted_iota(jnp.int32, qT.shape, 0)
    zero = jnp.zeros_like(qT)
    qs_ref[:, :tq] = jnp.where(row < DA_QK_DIM, qT, zero)
    qs_ref[:, tq:] = jnp.where(row >= DA_QK_DIM, qT, zero)

    n_tiles, n_full = _tile_counts(qi, tq=tq, tk=tk, nkv=nkv, q_off=q_off, kv_len=kv_len)
    _flash_tiles(n_tiles, n_full, lambda j: kb_ref[j], lambda j: vT_ref[j], lambda: qs_ref[...],
                 slice(0, DA_V_DIM), m_ref, l_ref, acc_ref,
                 qi=qi, tq=tq, tk=tk, q_off=q_off, kv_len=kv_len)

    lf = lam_ref[...]
    lam = (jnp.exp(jnp.sum(lf[0:1] * lf[1:2], axis=-1, keepdims=True))
           - jnp.exp(jnp.sum(lf[2:3] * lf[3:4], axis=-1, keepdims=True)) + lam_init)
    o = acc_ref[...] * (1.0 / l_ref[...])
    o = (o[:, :tq] - lam * o[:, tq:]).T
    o_ref[0] = (_rms(o, hg_ref[...]) * (1.0 - lam_init)).astype(BF16)


def _da_attention(lam, hg, qT, k, v, *, tq, tk, q_off, kv_len, lam_init):
    B, _, S = qT.shape
    T = k.shape[1]
    nkv = T // tk
    head_kv = lambda b, h, qi: (b, 0, h)
    return pl.pallas_call(
        functools.partial(_da_attn_kernel, tq=tq, tk=tk, nkv=nkv, q_off=q_off, kv_len=kv_len,
                          lam_init=lam_init),
        grid=(B, DA_HEADS, S // tq),
        in_specs=[
            pl.BlockSpec(lam.shape, lambda b, h, qi: (0, 0)),
            pl.BlockSpec((1, DA_V_DIM), lambda b, h, qi: (0, 0)),
            pl.BlockSpec((1, LANES, tq), lambda b, h, qi: (b, h, qi)),
            pl.BlockSpec((1, T, LANES), head_kv),
            pl.BlockSpec((1, T, LANES), head_kv),
        ],
        out_specs=pl.BlockSpec((1, tq, DA_V_DIM), lambda b, h, qi: (b, qi, h)),
        out_shape=jax.ShapeDtypeStruct((B, S, DA_COLS), BF16),
        scratch_shapes=[
            pltpu.VMEM((nkv, tk, LANES), BF16),
            pltpu.VMEM((nkv, DA_V_DIM, tk), BF16),
            pltpu.VMEM((LANES, 2 * tq), BF16),
            pltpu.VMEM((1, 2 * tq), F32),
            pltpu.VMEM((1, 2 * tq), F32),
            pltpu.VMEM((DA_V_DIM, 2 * tq), F32),
        ],
        compiler_params=pltpu.CompilerParams(
            dimension_semantics=("arbitrary", "arbitrary", "arbitrary"),
            vmem_limit_bytes=VMEM_LIMIT_BYTES),
        name="da_attention",
    )(lam, hg, qT, k, v)


MLA_PAIR = 2


def _mla_attn_kernel(qT_ref, k_ref, vT_ref, o_ref, m_ref, l_ref, acc_ref,
                     *, tq, tk, nkv, q_off, kv_len):
    qi = pl.program_id(2)
    n_tiles, n_full = _tile_counts(qi, tq=tq, tk=tk, nkv=nkv, q_off=q_off, kv_len=kv_len)
    for hh in range(MLA_PAIR):
        rows = slice(hh * MLA_V_DIM, (hh + 1) * MLA_V_DIM)
        cols = slice(hh * LANES, (hh + 1) * LANES)

        def load_k(j, cols=cols):
            return k_ref[0, pl.ds(pl.multiple_of(j * tk, tk), tk), cols]

        def load_vT(j, rows=rows):
            return vT_ref[0, j, rows, :]

        _flash_tiles(n_tiles, n_full, load_k, load_vT, lambda cols=cols: qT_ref[0, cols, :], rows, m_ref, l_ref, acc_ref,
                     qi=qi, tq=tq, tk=tk, q_off=q_off, kv_len=kv_len)
        acc_ref[rows, :] = acc_ref[rows, :] * (1.0 / l_ref[...])
    o_ref[0] = acc_ref[...].T.astype(BF16)


def _mla_attention(qmT, km, vT, *, tq, tk, q_off, kv_len):
    B, _, S = qmT.shape
    nkv = vT.shape[1]
    T = km.shape[1]
    return pl.pallas_call(
        functools.partial(_mla_attn_kernel, tq=tq, tk=tk, nkv=nkv, q_off=q_off, kv_len=kv_len),
        grid=(B, MLA_HEADS // MLA_PAIR, S // tq),
        in_specs=[
            pl.BlockSpec((1, MLA_PAIR * LANES, tq), lambda b, h, qi: (b, h, qi)),
            pl.BlockSpec((1, T, MLA_PAIR * LANES), lambda b, h, qi: (b, 0, h)),
            pl.BlockSpec((1, nkv, MLA_PAIR * MLA_V_DIM, tk), lambda b, h, qi: (b, 0, h, 0)),
        ],
        out_specs=pl.BlockSpec((1, tq, MLA_PAIR * MLA_V_DIM), lambda b, h, qi: (b, qi, h)),
        out_shape=jax.ShapeDtypeStruct((B, S, MLA_WIDTH), BF16),
        scratch_shapes=[
            pltpu.VMEM((1, tq), F32),
            pltpu.VMEM((1, tq), F32),
            pltpu.VMEM((MLA_PAIR * MLA_V_DIM, tq), F32),
        ],
        compiler_params=pltpu.CompilerParams(
            dimension_semantics=("arbitrary", "arbitrary", "arbitrary"),
            vmem_limit_bytes=VMEM_LIMIT_BYTES),
        name="mla_attention",
    )(qmT, km, vT)


def _epilogue_kernel(x_ref, oa_ref, ob_ref, ng_ref, wzg_ref, gb_ref, wa_ref, wb_ref, wo_ref, fg_ref,
                     y_ref, *, final_norm):
    x = x_ref[...]
    d = x.shape[1]
    h = _rms(x, ng_ref[...]).astype(BF16)
    zg = jnp.dot(h, wzg_ref[...], preferred_element_type=F32)
    ga = (oa_ref[...].astype(F32) * jax.nn.silu(zg[:, :DA_COLS])).astype(BF16)
    gb = (ob_ref[...].astype(F32) * jax.nn.silu(zg[:, DA_COLS:DA_COLS + MLA_WIDTH])).astype(BF16)
    ya = jnp.dot(ga, wa_ref[...], preferred_element_type=F32)
    yb = jnp.dot(gb, wb_ref[...], preferred_element_type=F32)
    g = jax.nn.sigmoid(zg[:, DA_COLS + MLA_WIDTH:] + gb_ref[...])
    m = (g[:, :d] * ya + g[:, d:] * yb).astype(BF16)
    out = x + jnp.dot(m, wo_ref[...], preferred_element_type=F32)
    y_ref[...] = _rms(out, fg_ref[...]) if final_norm else out


def _epilogue(x, oa, ob, norm_g, wzg, gate_b, wa, wb, wo, final_g, *, tm, final_norm):
    n, D = x.shape
    const = lambda i: (0, 0)
    tok = lambda i: (i, 0)
    return pl.pallas_call(
        functools.partial(_epilogue_kernel, final_norm=final_norm),
        grid=(n // tm,),
        in_specs=[
            pl.BlockSpec((tm, D), tok),
            pl.BlockSpec((tm, DA_COLS), tok),
            pl.BlockSpec((tm, MLA_WIDTH), tok),
            pl.BlockSpec((1, D), const),
            pl.BlockSpec(wzg.shape, const),
            pl.BlockSpec((1, 2 * D), const),
            pl.BlockSpec(wa.shape, const),
            pl.BlockSpec(wb.shape, const),
            pl.BlockSpec(wo.shape, const),
            pl.BlockSpec((1, D), const),
        ],
        out_specs=pl.BlockSpec((tm, D), tok),
        out_shape=jax.ShapeDtypeStruct((n, D), F32),
        compiler_params=pltpu.CompilerParams(
            dimension_semantics=("arbitrary",), vmem_limit_bytes=VMEM_LIMIT_BYTES),
        name="epilogue",
    )(x, oa, ob, norm_g, wzg, gate_b, wa, wb, wo, final_g)


def _tiles(S):
    t = min(512, S)
    assert S % t == 0 and t % LANES == 0
    return t


def _prep_weights(w_in, mla_w_uq, mla_w_uk):
    D = w_in.shape[0]
    sizes = (DA_COLS, DA_COLS, DA_COLS, DA_COLS, MLA_Q_LORA, MLA_KV_LORA, MLA_ROPE, MLA_WIDTH, 2 * D)
    assert w_in.shape[1] == sum(sizes)
    w_q, w_k, w_v, w_za, w_cq, w_ckv, w_kr, w_zb, w_g = jnp.split(w_in, np.cumsum(sizes)[:-1].tolist(), axis=1)
    w_kr = jnp.pad(w_kr, ((0, 0), (MLA_NOPE, LANES - MLA_NOPE - MLA_ROPE)))
    w_main = jnp.concatenate([w_q, w_k, w_v, w_cq, w_ckv, w_kr], axis=1).astype(BF16)
    w_zg = jnp.concatenate([w_za, w_zb, w_g], axis=1).astype(BF16)
    pad_heads = lambda w, dh: jnp.pad(
        w.reshape(w.shape[0], MLA_HEADS, dh), ((0, 0), (0, 0), (0, LANES - dh))
    ).reshape(w.shape[0], MLA_QK_COLS).astype(BF16)
    return w_main, w_zg, pad_heads(mla_w_uq, MLA_NOPE + MLA_ROPE), pad_heads(mla_w_uk, MLA_NOPE)


def _layer(x, past, lam_init, final_norm, norm_g, w_in, gate_b, da_lambda, da_head_norm_g, mla_q_norm_g,
           mla_kv_norm_g, mla_w_uq, mla_w_uk, mla_w_uv, w_branch_a, w_branch_b, w_out, final_norm_g):
    B, S, D = x.shape
    past_len = 0 if past is None else past[0].shape[1]
    Sp = -(-S // LANES) * LANES
    xp = jnp.pad(x, ((0, 0), (0, Sp - S), (0, 0))) if Sp != S else x
    tm = _tiles(Sp)
    pos = past_len + jnp.arange(Sp, dtype=jnp.int32)
    tda = _rope_tables(pos, DA_ROT, _da_lane)
    tml = _rope_tables(pos, MLA_ROPE, _mla_lane)
    w_main, w_zg, wuq, wuk = _prep_weights(w_in, mla_w_uq, mla_w_uk)
    row = lambda v: v.reshape(1, -1)

    qT, k_new, v_new, lat_new, kr_new, qmT = _inproj(
        xp, row(norm_g), w_main, row(mla_q_norm_g), row(mla_kv_norm_g), wuq, tda, tml, tm=tm)
    k_new = k_new.reshape(B, Sp, DA_COLS)
    v_new = v_new.reshape(B, Sp, DA_COLS)
    lat_new = lat_new.reshape(B, Sp, MLA_KV_LORA)
    kr_new = kr_new.reshape(B, Sp, LANES)

    kv_len = past_len + S
    if past is None:
        k_all, v_all, c_all, kr_all = k_new, v_new, lat_new, kr_new
        tk = tm
    else:
        pk, pv, pc, pr = past
        tk = -(-kv_len // LANES) * LANES
        cat = lambda old, new: jnp.pad(jnp.concatenate([old, new[:, :S]], axis=1),
                                       ((0, 0), (0, tk - kv_len), (0, 0)))
        k_all = cat(pk.reshape(B, past_len, DA_COLS), k_new)
        v_all = cat(pv.reshape(B, past_len, DA_COLS), v_new)
        c_all = cat(pc, lat_new)
        kr_all = cat(jnp.pad(pr, ((0, 0), (0, 0), (MLA_NOPE, LANES - MLA_NOPE - MLA_ROPE))), kr_new)

    o_a = _da_attention(da_lambda, row(da_head_norm_g), qT, k_all, v_all,
                        tq=tm, tk=tk, q_off=past_len, kv_len=kv_len, lam_init=lam_init)
    km, vmT = _expand(c_all, kr_all, wuk, mla_w_uv.astype(BF16), tk=tk)
    o_b = _mla_attention(qmT, km, vmT, tq=tm, tk=tk, q_off=past_len, kv_len=kv_len)

    n = B * Sp
    y = _epilogue(xp.reshape(n, D), o_a.reshape(n, DA_COLS), o_b.reshape(n, MLA_WIDTH), row(norm_g), w_zg,
                  row(gate_b), w_branch_a.astype(BF16), w_branch_b.astype(BF16), w_out.astype(BF16),
                  row(final_norm_g), tm=_tiles(n), final_norm=final_norm)
    y = y.reshape(B, Sp, D)[:, :S]
    new = (k_new[:, :S].reshape(B, S, DA_HEADS, DA_V_DIM), v_new[:, :S].reshape(B, S, DA_HEADS, DA_V_DIM),
           lat_new[:, :S], kr_new[:, :S, MLA_NOPE:MLA_NOPE + MLA_ROPE])
    return y, new


def kernel(x_prompt, x_sample, cache_da_k, cache_da_v, cache_mla_latent, cache_mla_krope, norm_g, w_in, gate_b, da_lambda, da_head_norm_g, mla_q_norm_g, mla_kv_norm_g, mla_w_uq, mla_w_uk, mla_w_uv, w_branch_a, w_branch_b, w_out, final_norm_g):
    depth = w_in.shape[0]
    hp, hs = x_prompt, x_sample
    rows_p, rows_s = [], []
    for l in range(depth):
        lam_init = 0.8 - 0.6 * math.exp(-0.3 * l)
        last = l == depth - 1
        w = (norm_g[l], w_in[l], gate_b[l], da_lambda[l], da_head_norm_g[l], mla_q_norm_g[l], mla_kv_norm_g[l],
             mla_w_uq[l], mla_w_uk[l], mla_w_uv[l], w_branch_a[l], w_branch_b[l], w_out[l], final_norm_g)
        hp, new_p = _layer(hp, None, lam_init, last, *w)
        past = (cache_da_k[l], cache_da_v[l], cache_mla_latent[l], cache_mla_krope[l])
        hs, new_s = _layer(hs, past, lam_init, last, *w)
        rows_p.append(new_p)
        rows_s.append(new_s)
    stack = lambda rows, i: jnp.stack([r[i] for r in rows], 0)
    return (hp, hs, stack(rows_p, 0), stack(rows_p, 1), stack(rows_p, 2), stack(rows_p, 3),
            stack(rows_s, 0), stack(rows_s, 1), stack(rows_s, 2), stack(rows_s, 3))
```

```python
import functools
import math
from typing import Callable, NamedTuple

import jax
import jax.numpy as jnp
import numpy as np
from jax import lax
from jax.experimental import pallas as pl
from jax.experimental.pallas import tpu as pltpu

CHUNK = 64
ROPE_THETA = 500000.0
RMS_EPS = 1e-6
DA_HEADS = 4
DA_QK_DIM = 64
DA_V_DIM = 2 * DA_QK_DIM
DA_ROT = DA_QK_DIM // 4
DA_COLS = DA_HEADS * DA_V_DIM
MLA_HEADS = 8
MLA_Q_LORA = 384
MLA_KV_LORA = 256
MLA_NOPE = 64
MLA_ROPE = 32
MLA_V_DIM = 64
MLA_WIDTH = MLA_HEADS * MLA_V_DIM

LANES = 128
SUM_ROWS = 8
BF16_ROWS = 16
MLA_QK_COLS = MLA_HEADS * LANES
LOG2E = math.log2(math.e)
NEG_BIG = -1e30
VMEM_LIMIT_BYTES = 56 * 1024 * 1024

F32 = jnp.float32
BF16 = jnp.bfloat16


def _rms(x, g):
    return x * lax.rsqrt(jnp.mean(x * x, axis=-1, keepdims=True) + RMS_EPS) * g


def _rope128(x, tab_ref, shift):
    return (x * tab_ref[0] + pltpu.roll(x, LANES - shift, 1) * tab_ref[1]
            + pltpu.roll(x, shift, 1) * tab_ref[2])


def _rope_tables(pos, rot, lane_of):
    half = rot // 2
    inv = jnp.float32(ROPE_THETA) ** (-jnp.arange(half, dtype=F32) * 2.0 / rot)
    ang = pos.astype(F32)[:, None] * inv
    cos, sin = jnp.cos(ang), jnp.sin(ang)
    idx = np.array([lane_of(j) for j in range(LANES)])
    lo = (idx >= 0) & (idx < half)
    hi = idx >= half
    src = np.where(idx >= 0, idx % half, 0)
    c = jnp.where(jnp.asarray(lo | hi), cos[:, src], 1.0)
    sa = jnp.where(jnp.asarray(lo), -sin[:, src], 0.0)
    sb = jnp.where(jnp.asarray(hi), sin[:, src], 0.0)
    return jnp.stack([c, sa, sb]).astype(F32)


def _da_lane(j):
    c = j % DA_QK_DIM
    return c if c < DA_ROT else -1


def _mla_lane(j):
    c = j - MLA_NOPE
    return c if 0 <= c < MLA_ROPE else -1


def _inproj_kernel(x_ref, ng_ref, wm_ref, qg_ref, kvg_ref, wuq_ref, tda_ref, tml_ref,
                   qT_ref, k_ref, v_ref, lat_ref, kr_ref, qmT_ref, *, q_scale, qm_scale):
    h = _rms(x_ref[...], ng_ref[...]).astype(BF16)
    proj = jnp.dot(h, wm_ref[...], preferred_element_type=F32)
    o_k, o_v = DA_COLS, 2 * DA_COLS
    o_cq = 3 * DA_COLS
    o_ckv = o_cq + MLA_Q_LORA
    o_kr = o_ckv + MLA_KV_LORA
    for i in range(DA_HEADS):
        lo, hi = i * LANES, (i + 1) * LANES
        q = _rope128(proj[:, lo:hi], tda_ref, DA_ROT // 2) * q_scale
        qT_ref[0, lo:hi, :] = q.T.astype(BF16)
        k_ref[:, lo:hi] = _rope128(proj[:, o_k + lo:o_k + hi], tda_ref, DA_ROT // 2)
    v_ref[...] = proj[:, o_v:o_v + DA_COLS]
    lat_ref[...] = _rms(proj[:, o_ckv:o_ckv + MLA_KV_LORA], kvg_ref[...])
    kr_ref[...] = _rope128(proj[:, o_kr:o_kr + LANES], tml_ref, MLA_ROPE // 2)
    qn = _rms(proj[:, o_cq:o_cq + MLA_Q_LORA], qg_ref[...]).astype(BF16)
    qm = jnp.dot(qn, wuq_ref[...], preferred_element_type=F32)
    for i in range(MLA_HEADS):
        lo, hi = i * LANES, (i + 1) * LANES
        q = _rope128(qm[:, lo:hi], tml_ref, MLA_ROPE // 2) * qm_scale
        qmT_ref[0, lo:hi, :] = q.T.astype(BF16)


def _inproj(x, norm_g, w_main, qg, kvg, wuq, tda, tml, *, tm):
    B, S, D = x.shape
    n = B * S
    nst = S // tm
    wcols = w_main.shape[1]
    const = lambda i: (0, 0)
    tok = lambda i: (i, 0)
    feat_t = lambda i: (i // nst, 0, i % nst)
    tab = lambda i: (0, i % nst, 0)
    return pl.pallas_call(
        functools.partial(_inproj_kernel, q_scale=DA_QK_DIM ** -0.5 * LOG2E,
                          qm_scale=(MLA_NOPE + MLA_ROPE) ** -0.5 * LOG2E),
        grid=(n // tm,),
        in_specs=[
            pl.BlockSpec((tm, D), tok),
            pl.BlockSpec((1, D), const),
            pl.BlockSpec((D, wcols), const),
            pl.BlockSpec((1, MLA_Q_LORA), const),
            pl.BlockSpec((1, MLA_KV_LORA), const),
            pl.BlockSpec((MLA_Q_LORA, MLA_QK_COLS), const),
            pl.BlockSpec((3, tm, LANES), tab),
            pl.BlockSpec((3, tm, LANES), tab),
        ],
        out_specs=[
            pl.BlockSpec((1, DA_COLS, tm), feat_t),
            pl.BlockSpec((tm, DA_COLS), tok),
            pl.BlockSpec((tm, DA_COLS), tok),
            pl.BlockSpec((tm, MLA_KV_LORA), tok),
            pl.BlockSpec((tm, LANES), tok),
            pl.BlockSpec((1, MLA_QK_COLS, tm), feat_t),
        ],
        out_shape=[
            jax.ShapeDtypeStruct((B, DA_COLS, S), BF16),
            jax.ShapeDtypeStruct((n, DA_COLS), F32),
            jax.ShapeDtypeStruct((n, DA_COLS), F32),
            jax.ShapeDtypeStruct((n, MLA_KV_LORA), F32),
            jax.ShapeDtypeStruct((n, LANES), F32),
            jax.ShapeDtypeStruct((B, MLA_QK_COLS, S), BF16),
        ],
        compiler_params=pltpu.CompilerParams(
            dimension_semantics=("arbitrary",), vmem_limit_bytes=VMEM_LIMIT_BYTES),
        name="inproj",
    )(x.reshape(n, D), norm_g, w_main, qg, kvg, wuq, tda, tml)


def _expand_kernel(c_ref, kr_ref, wuk_ref, wuv_ref, km_ref, vT_ref):
    c = c_ref[0].astype(BF16)
    kn = jnp.dot(c, wuk_ref[...], preferred_element_type=F32)
    kr = kr_ref[0]
    for i in range(MLA_HEADS):
        lo, hi = i * LANES, (i + 1) * LANES
        km_ref[0, :, lo:hi] = (kn[:, lo:hi] + kr).astype(BF16)
    vm = jnp.dot(c, wuv_ref[...], preferred_element_type=F32)
    for i in range(MLA_WIDTH // LANES):
        lo, hi = i * LANES, (i + 1) * LANES
        vT_ref[0, 0, lo:hi, :] = vm[:, lo:hi].T.astype(BF16)


def _expand(c, kr, wuk, wuv, *, tk):
    B, T, _ = c.shape
    nkv = T // tk
    const = lambda b, j: (0, 0)
    tile = lambda b, j: (b, j, 0)
    return pl.pallas_call(
        _expand_kernel,
        grid=(B, nkv),
        in_specs=[
            pl.BlockSpec((1, tk, MLA_KV_LORA), tile),
            pl.BlockSpec((1, tk, LANES), tile),
            pl.BlockSpec((MLA_KV_LORA, MLA_QK_COLS), const),
            pl.BlockSpec((MLA_KV_LORA, MLA_WIDTH), const),
        ],
        out_specs=[
            pl.BlockSpec((1, tk, MLA_QK_COLS), tile),
            pl.BlockSpec((1, 1, MLA_WIDTH, tk), lambda b, j: (b, j, 0, 0)),
        ],
        out_shape=[
            jax.ShapeDtypeStruct((B, T, MLA_QK_COLS), BF16),
            jax.ShapeDtypeStruct((B, nkv, MLA_WIDTH, tk), BF16),
        ],
        compiler_params=pltpu.CompilerParams(
            dimension_semantics=("arbitrary", "arbitrary"), vmem_limit_bytes=VMEM_LIMIT_BYTES),
        name="kv_expand",
    )(c, kr, wuk, wuv)


def _check_tiling(*, tq, tk, nkv, q_off, kv_len):
    assert nkv == 1 or (q_off == 0 and tq == tk and kv_len == nkv * tk), (tq, tk, nkv, q_off, kv_len)


class _Chain(NamedTuple):
    load_k: Callable
    load_vT: Callable
    load_q: Callable
    s: object
    mx: object
    p: object
    a: object
    m: object
    l: object
    acc: object


def _flash_tiles(n_full, chains, *, qi, tq, tk, q_off, kv_len):
    n = n_full + 1
    for c in chains:
        c.m[...] = jnp.full(c.m.shape, NEG_BIG, F32)
        c.l[...] = jnp.zeros(c.l.shape, F32)
        c.acc[...] = jnp.zeros(c.acc.shape, F32)

    def tile_of(i):
        return jnp.where(i == 0, n_full, i - 1)

    def scores(i, slot, masked=False):
        for c in chains:
            ncols = c.m.shape[1]
            s = jnp.dot(c.load_k(tile_of(i)), c.load_q(), preferred_element_type=F32)
            if masked:
                kpos = n_full * tk + lax.broadcasted_iota(jnp.int32, (tk, 1), 0)
                col = lax.broadcasted_iota(jnp.int32, (1, ncols), 1)
                qpos = q_off + qi * tq + jnp.where(col >= tq, col - tq, col)
                ok = ((lax.shift_right_logical(kpos, 6) <= lax.shift_right_logical(qpos, 6))
                      & (kpos < kv_len))
                s = jnp.where(ok, s, NEG_BIG)
            c.s[slot] = s
            c.mx[slot] = jnp.max(s, axis=0, keepdims=True)

    def softmax(slot):
        for c in chains:
            ncols = c.m.shape[1]
            m_prev = c.m[...]
            m_new = jnp.maximum(m_prev, c.mx[slot])
            alpha = jnp.exp2(m_prev - m_new)
            c.m[...] = m_new
            c.a[slot] = alpha
            part = jnp.zeros((SUM_ROWS, ncols), F32)
            for r in range(0, tk, BF16_ROWS):
                p = jnp.exp2(c.s[slot, r:r + BF16_ROWS, :] - m_new)
                c.p[slot, r:r + BF16_ROWS, :] = p.astype(BF16)
                for h in range(0, BF16_ROWS, SUM_ROWS):
                    part = part + p[h:h + SUM_ROWS]
            c.l[...] = alpha * c.l[...] + jnp.sum(part, axis=0, keepdims=True)

    def values(i, slot):
        for c in chains:
            pv = jnp.dot(c.load_vT(tile_of(i)), c.p[slot], preferred_element_type=F32)
            c.acc[...] = c.a[slot] * c.acc[...] + pv

    def stage(i, cur, nxt):
        scores(i + 1, nxt)
        softmax(cur)
        values(i - 1, nxt)

    scores(0, 0, masked=True)
    scores(jnp.minimum(1, n - 1), 1)
    softmax(0)

    def pair(t, carry):
        i = 1 + 2 * t
        stage(i, 1, 0)
        stage(i + 1, 0, 1)
        return carry

    lax.fori_loop(0, lax.shift_right_arithmetic(n - 2, 1), pair, 0)
    odd = lax.rem(n, 2) == 1

    @pl.when(jnp.logical_and(odd, n >= 3))
    def _():
        stage(n - 2, 1, 0)
        softmax(0)
        values(n - 2, 1)
        values(n - 1, 0)

    @pl.when(jnp.logical_not(odd))
    def _():
        softmax(1)
        values(n - 2, 0)
        values(n - 1, 1)

    @pl.when(n == 1)
    def _():
        values(0, 0)


def _last_tile(qi, *, tq, tk, nkv, q_off):
    return jnp.minimum(nkv - 1, (q_off + (qi + 1) * tq - 1) // tk)


HEAD_PAIR = 2


def _da_attn_kernel(lam_ref, hg_ref, qT_ref, k_ref, v_ref, o_ref,
                    kb_ref, vT_ref, qs_ref, s_ref, mx_ref, p_ref, a_ref, m_ref, l_ref, acc_ref,
                    *, tq, tk, nkv, q_off, kv_len, lam_init):
    qi = pl.program_id(2)

    @pl.when(qi == 0)
    def _():
        for c in range(nkv):
            kb_ref[c] = k_ref[0, c * tk:(c + 1) * tk, :].astype(BF16)
            for hh in range(HEAD_PAIR):
                vT_ref[hh, c] = v_ref[0, c * tk:(c + 1) * tk, hh * LANES:(hh + 1) * LANES].T.astype(BF16)

    row = lax.broadcasted_iota(jnp.int32, (LANES, tq), 0)
    zero = jnp.zeros((LANES, tq), BF16)
    chains = []
    for hh in range(HEAD_PAIR):
        cols = slice(hh * LANES, (hh + 1) * LANES)
        qT = qT_ref[0, cols, :]
        qs_ref[hh, :, :tq] = jnp.where(row < DA_QK_DIM, qT, zero)
        qs_ref[hh, :, tq:] = jnp.where(row >= DA_QK_DIM, qT, zero)
        chains.append(_Chain(
            load_k=lambda j, cols=cols: kb_ref[j, :, cols],
            load_vT=lambda j, hh=hh: vT_ref[hh, j],
            load_q=lambda hh=hh: qs_ref[hh],
            s=s_ref.at[hh], mx=mx_ref.at[hh], p=p_ref.at[hh], a=a_ref.at[hh], m=m_ref.at[hh],
            l=l_ref.at[hh],
            acc=acc_ref.at[hh]))

    n_full = _last_tile(qi, tq=tq, tk=tk, nkv=nkv, q_off=q_off)
    _flash_tiles(n_full, chains, qi=qi, tq=tq, tk=tk, q_off=q_off, kv_len=kv_len)

    lf = lam_ref[...]
    lam = (jnp.exp(jnp.sum(lf[0:1] * lf[1:2], axis=-1, keepdims=True))
           - jnp.exp(jnp.sum(lf[2:3] * lf[3:4], axis=-1, keepdims=True)) + lam_init)
    for hh in range(HEAD_PAIR):
        o = acc_ref[hh] * (1.0 / l_ref[hh])
        o = (o[:, :tq] - lam * o[:, tq:]).T
        o_ref[0, :, hh * LANES:(hh + 1) * LANES] = (
            _rms(o, hg_ref[...]) * (1.0 - lam_init)).astype(BF16)


def _da_attention(lam, hg, qT, k, v, *, tq, tk, q_off, kv_len, lam_init):
    B, _, S = qT.shape
    T = k.shape[1]
    nkv = T // tk
    _check_tiling(tq=tq, tk=tk, nkv=nkv, q_off=q_off, kv_len=kv_len)
    pair = HEAD_PAIR * LANES
    head_kv = lambda b, h, qi: (b, 0, h)
    return pl.pallas_call(
        functools.partial(_da_attn_kernel, tq=tq, tk=tk, nkv=nkv, q_off=q_off, kv_len=kv_len,
                          lam_init=lam_init),
        grid=(B, DA_HEADS // HEAD_PAIR, S // tq),
        in_specs=[
            pl.BlockSpec(lam.shape, lambda b, h, qi: (0, 0)),
            pl.BlockSpec((1, DA_V_DIM), lambda b, h, qi: (0, 0)),
            pl.BlockSpec((1, pair, tq), lambda b, h, qi: (b, h, qi)),
            pl.BlockSpec((1, T, pair), head_kv),
            pl.BlockSpec((1, T, pair), head_kv),
        ],
        out_specs=pl.BlockSpec((1, tq, pair), lambda b, h, qi: (b, qi, h)),
        out_shape=jax.ShapeDtypeStruct((B, S, DA_COLS), BF16),
        scratch_shapes=[
            pltpu.VMEM((nkv, tk, pair), BF16),
            pltpu.VMEM((HEAD_PAIR, nkv, DA_V_DIM, tk), BF16),
            pltpu.VMEM((HEAD_PAIR, LANES, 2 * tq), BF16),
            pltpu.VMEM((HEAD_PAIR, 2, tk, 2 * tq), F32),
            pltpu.VMEM((HEAD_PAIR, 2, 1, 2 * tq), F32),
            pltpu.VMEM((HEAD_PAIR, 2, tk, 2 * tq), BF16),
            pltpu.VMEM((HEAD_PAIR, 2, 1, 2 * tq), F32),
            pltpu.VMEM((HEAD_PAIR, 1, 2 * tq), F32),
            pltpu.VMEM((HEAD_PAIR, 1, 2 * tq), F32),
            pltpu.VMEM((HEAD_PAIR, DA_V_DIM, 2 * tq), F32),
        ],
        compiler_params=pltpu.CompilerParams(
            dimension_semantics=("arbitrary", "arbitrary", "arbitrary"),
            vmem_limit_bytes=VMEM_LIMIT_BYTES),
        name="da_attention",
    )(lam, hg, qT, k, v)


def _mla_attn_kernel(qT_ref, k_ref, vT_ref, o_ref, s_ref, mx_ref, p_ref, a_ref, m_ref, l_ref, acc_ref,
                     *, tq, tk, nkv, q_off, kv_len):
    qi = pl.program_id(2)
    chains = []
    for hh in range(HEAD_PAIR):
        rows = slice(hh * MLA_V_DIM, (hh + 1) * MLA_V_DIM)
        cols = slice(hh * LANES, (hh + 1) * LANES)
        chains.append(_Chain(
            load_k=lambda j, cols=cols: k_ref[0, pl.ds(pl.multiple_of(j * tk, tk), tk), cols],
            load_vT=lambda j, rows=rows: vT_ref[0, j, rows, :],
            load_q=lambda cols=cols: qT_ref[0, cols, :],
            s=s_ref.at[hh], mx=mx_ref.at[hh], p=p_ref.at[hh], a=a_ref.at[hh], m=m_ref.at[hh],
            l=l_ref.at[hh],
            acc=acc_ref.at[rows]))

    n_full = _last_tile(qi, tq=tq, tk=tk, nkv=nkv, q_off=q_off)
    _flash_tiles(n_full, chains, qi=qi, tq=tq, tk=tk, q_off=q_off, kv_len=kv_len)
    for c in chains:
        c.acc[...] = c.acc[...] * (1.0 / c.l[...])
    o_ref[0] = acc_ref[...].T.astype(BF16)


def _mla_attention(qmT, km, vT, *, tq, tk, q_off, kv_len):
    B, _, S = qmT.shape
    nkv = vT.shape[1]
    T = km.shape[1]
    _check_tiling(tq=tq, tk=tk, nkv=nkv, q_off=q_off, kv_len=kv_len)
    return pl.pallas_call(
        functools.partial(_mla_attn_kernel, tq=tq, tk=tk, nkv=nkv, q_off=q_off, kv_len=kv_len),
        grid=(B, MLA_HEADS // HEAD_PAIR, S // tq),
        in_specs=[
            pl.BlockSpec((1, HEAD_PAIR * LANES, tq), lambda b, h, qi: (b, h, qi)),
            pl.BlockSpec((1, T, HEAD_PAIR * LANES), lambda b, h, qi: (b, 0, h)),
            pl.BlockSpec((1, nkv, HEAD_PAIR * MLA_V_DIM, tk), lambda b, h, qi: (b, 0, h, 0)),
        ],
        out_specs=pl.BlockSpec((1, tq, HEAD_PAIR * MLA_V_DIM), lambda b, h, qi: (b, qi, h)),
        out_shape=jax.ShapeDtypeStruct((B, S, MLA_WIDTH), BF16),
        scratch_shapes=[
            pltpu.VMEM((HEAD_PAIR, 2, tk, tq), F32),
            pltpu.VMEM((HEAD_PAIR, 2, 1, tq), F32),
            pltpu.VMEM((HEAD_PAIR, 2, tk, tq), BF16),
            pltpu.VMEM((HEAD_PAIR, 2, 1, tq), F32),
            pltpu.VMEM((HEAD_PAIR, 1, tq), F32),
            pltpu.VMEM((HEAD_PAIR, 1, tq), F32),
            pltpu.VMEM((HEAD_PAIR * MLA_V_DIM, tq), F32),
        ],
        compiler_params=pltpu.CompilerParams(
            dimension_semantics=("arbitrary", "arbitrary", "arbitrary"),
            vmem_limit_bytes=VMEM_LIMIT_BYTES),
        name="mla_attention",
    )(qmT, km, vT)


def _epilogue_kernel(x_ref, oa_ref, ob_ref, ng_ref, wzg_ref, gb_ref, wa_ref, wb_ref, wo_ref, fg_ref,
                     y_ref, *, final_norm):
    x = x_ref[...]
    d = x.shape[1]
    h = _rms(x, ng_ref[...]).astype(BF16)
    zg = jnp.dot(h, wzg_ref[...], preferred_element_type=F32)
    ga = (oa_ref[...].astype(F32) * jax.nn.silu(zg[:, :DA_COLS])).astype(BF16)
    gb = (ob_ref[...].astype(F32) * jax.nn.silu(zg[:, DA_COLS:DA_COLS + MLA_WIDTH])).astype(BF16)
    ya = jnp.dot(ga, wa_ref[...], preferred_element_type=F32)
    yb = jnp.dot(gb, wb_ref[...], preferred_element_type=F32)
    g = jax.nn.sigmoid(zg[:, DA_COLS + MLA_WIDTH:] + gb_ref[...])
    m = (g[:, :d] * ya + g[:, d:] * yb).astype(BF16)
    out = x + jnp.dot(m, wo_ref[...], preferred_element_type=F32)
    y_ref[...] = _rms(out, fg_ref[...]) if final_norm else out


def _epilogue(x, oa, ob, norm_g, wzg, gate_b, wa, wb, wo, final_g, *, tm, final_norm):
    n, D = x.shape
    const = lambda i: (0, 0)
    tok = lambda i: (i, 0)
    return pl.pallas_call(
        functools.partial(_epilogue_kernel, final_norm=final_norm),
        grid=(n // tm,),
        in_specs=[
            pl.BlockSpec((tm, D), tok),
            pl.BlockSpec((tm, DA_COLS), tok),
            pl.BlockSpec((tm, MLA_WIDTH), tok),
            pl.BlockSpec((1, D), const),
            pl.BlockSpec(wzg.shape, const),
            pl.BlockSpec((1, 2 * D), const),
            pl.BlockSpec(wa.shape, const),
            pl.BlockSpec(wb.shape, const),
            pl.BlockSpec(wo.shape, const),
            pl.BlockSpec((1, D), const),
        ],
        out_specs=pl.BlockSpec((tm, D), tok),
        out_shape=jax.ShapeDtypeStruct((n, D), F32),
        compiler_params=pltpu.CompilerParams(
            dimension_semantics=("arbitrary",), vmem_limit_bytes=VMEM_LIMIT_BYTES),
        name="epilogue",
    )(x, oa, ob, norm_g, wzg, gate_b, wa, wb, wo, final_g)


def _tiles(S):
    t = min(512, S)
    assert S % t == 0 and t % LANES == 0
    return t


def _prep_weights(w_in, mla_w_uq, mla_w_uk):
    D = w_in.shape[0]
    sizes = (DA_COLS, DA_COLS, DA_COLS, DA_COLS, MLA_Q_LORA, MLA_KV_LORA, MLA_ROPE, MLA_WIDTH, 2 * D)
    assert w_in.shape[1] == sum(sizes)
    w_q, w_k, w_v, w_za, w_cq, w_ckv, w_kr, w_zb, w_g = jnp.split(w_in, np.cumsum(sizes)[:-1].tolist(), axis=1)
    w_kr = jnp.pad(w_kr, ((0, 0), (MLA_NOPE, LANES - MLA_NOPE - MLA_ROPE)))
    w_main = jnp.concatenate([w_q, w_k, w_v, w_cq, w_ckv, w_kr], axis=1).astype(BF16)
    w_zg = jnp.concatenate([w_za, w_zb, w_g], axis=1).astype(BF16)
    pad_heads = lambda w, dh: jnp.pad(
        w.reshape(w.shape[0], MLA_HEADS, dh), ((0, 0), (0, 0), (0, LANES - dh))
    ).reshape(w.shape[0], MLA_QK_COLS).astype(BF16)
    return w_main, w_zg, pad_heads(mla_w_uq, MLA_NOPE + MLA_ROPE), pad_heads(mla_w_uk, MLA_NOPE)


def _layer(x, past, lam_init, final_norm, norm_g, w_in, gate_b, da_lambda, da_head_norm_g, mla_q_norm_g,
           mla_kv_norm_g, mla_w_uq, mla_w_uk, mla_w_uv, w_branch_a, w_branch_b, w_out, final_norm_g):
    B, S, D = x.shape
    past_len = 0 if past is None else past[0].shape[1]
    Sp = -(-S // LANES) * LANES
    xp = jnp.pad(x, ((0, 0), (0, Sp - S), (0, 0))) if Sp != S else x
    tm = _tiles(Sp)
    pos = past_len + jnp.arange(Sp, dtype=jnp.int32)
    tda = _rope_tables(pos, DA_ROT, _da_lane)
    tml = _rope_tables(pos, MLA_ROPE, _mla_lane)
    w_main, w_zg, wuq, wuk = _prep_weights(w_in, mla_w_uq, mla_w_uk)
    row = lambda v: v.reshape(1, -1)

    qT, k_new, v_new, lat_new, kr_new, qmT = _inproj(
        xp, row(norm_g), w_main, row(mla_q_norm_g), row(mla_kv_norm_g), wuq, tda, tml, tm=tm)
    k_new = k_new.reshape(B, Sp, DA_COLS)
    v_new = v_new.reshape(B, Sp, DA_COLS)
    lat_new = lat_new.reshape(B, Sp, MLA_KV_LORA)
    kr_new = kr_new.reshape(B, Sp, LANES)

    kv_len = past_len + S
    if past is None:
        k_all, v_all, c_all, kr_all = k_new, v_new, lat_new, kr_new
        tk = tm
    else:
        pk, pv, pc, pr = past
        tk = -(-kv_len // LANES) * LANES
        cat = lambda old, new: jnp.pad(jnp.concatenate([old, new[:, :S]], axis=1),
                                       ((0, 0), (0, tk - kv_len), (0, 0)))
        k_all = cat(pk.reshape(B, past_len, DA_COLS), k_new)
        v_all = cat(pv.reshape(B, past_len, DA_COLS), v_new)
        c_all = cat(pc, lat_new)
        kr_all = cat(jnp.pad(pr, ((0, 0), (0, 0), (MLA_NOPE, LANES - MLA_NOPE - MLA_ROPE))), kr_new)

    o_a = _da_attention(da_lambda, row(da_head_norm_g), qT, k_all, v_all,
                        tq=tm, tk=tk, q_off=past_len, kv_len=kv_len, lam_init=lam_init)
    km, vmT = _expand(c_all, kr_all, wuk, mla_w_uv.astype(BF16), tk=tk)
    o_b = _mla_attention(qmT, km, vmT, tq=tm, tk=tk, q_off=past_len, kv_len=kv_len)

    n = B * Sp
    y = _epilogue(xp.reshape(n, D), o_a.reshape(n, DA_COLS), o_b.reshape(n, MLA_WIDTH), row(norm_g), w_zg,
                  row(gate_b), w_branch_a.astype(BF16), w_branch_b.astype(BF16), w_out.astype(BF16),
                  row(final_norm_g), tm=_tiles(n), final_norm=final_norm)
    y = y.reshape(B, Sp, D)[:, :S]
    new = (k_new[:, :S].reshape(B, S, DA_HEADS, DA_V_DIM), v_new[:, :S].reshape(B, S, DA_HEADS, DA_V_DIM),
           lat_new[:, :S], kr_new[:, :S, MLA_NOPE:MLA_NOPE + MLA_ROPE])
    return y, new


def kernel(x_prompt, x_sample, cache_da_k, cache_da_v, cache_mla_latent, cache_mla_krope, norm_g, w_in, gate_b, da_lambda, da_head_norm_g, mla_q_norm_g, mla_kv_norm_g, mla_w_uq, mla_w_uk, mla_w_uv, w_branch_a, w_branch_b, w_out, final_norm_g):
    depth = w_in.shape[0]
    hp, hs = x_prompt, x_sample
    rows_p, rows_s = [], []
    for l in range(depth):
        lam_init = 0.8 - 0.6 * math.exp(-0.3 * l)
        last = l == depth - 1
        w = (norm_g[l], w_in[l], gate_b[l], da_lambda[l], da_head_norm_g[l], mla_q_norm_g[l], mla_kv_norm_g[l],
             mla_w_uq[l], mla_w_uk[l], mla_w_uv[l], w_branch_a[l], w_branch_b[l], w_out[l], final_norm_g)
        hp, new_p = _layer(hp, None, lam_init, last, *w)
        past = (cache_da_k[l], cache_da_v[l], cache_mla_latent[l], cache_mla_krope[l])
        hs, new_s = _layer(hs, past, lam_init, last, *w)
        rows_p.append(new_p)
        rows_s.append(new_s)
    stack = lambda rows, i: jnp.stack([r[i] for r in rows], 0)
    return (hp, hs, stack(rows_p, 0), stack(rows_p, 1), stack(rows_p, 2), stack(rows_p, 3),
            stack(rows_s, 0), stack(rows_s, 1), stack(rows_s, 2), stack(rows_s, 3))
```

```python
import functools
import math
from typing import Callable, NamedTuple

import jax
import jax.numpy as jnp
import numpy as np
from jax import lax
from jax.experimental import pallas as pl
from jax.experimental.pallas import tpu as pltpu

CHUNK = 64
ROPE_THETA = 500000.0
RMS_EPS = 1e-6
DA_HEADS = 4
DA_QK_DIM = 64
DA_V_DIM = 2 * DA_QK_DIM
DA_ROT = DA_QK_DIM // 4
DA_COLS = DA_HEADS * DA_V_DIM
MLA_HEADS = 8
MLA_Q_LORA = 384
MLA_KV_LORA = 256
MLA_NOPE = 64
MLA_ROPE = 32
MLA_V_DIM = 64
MLA_WIDTH = MLA_HEADS * MLA_V_DIM
MLA_QK_DIM = MLA_NOPE + MLA_ROPE

LANES = 128
SUM_ROWS = 8
BF16_ROWS = 16
MLA_QK_COLS = MLA_HEADS * LANES
MLA_FEAT = LANES - MLA_QK_DIM
LOG2E = math.log2(math.e)
NEG_BIG = -1e30
VMEM_LIMIT_BYTES = 56 * 1024 * 1024
HEAD_PAIR = 2

F32 = jnp.float32
BF16 = jnp.bfloat16


def _rms(x, g):
    return x * lax.rsqrt(jnp.mean(x * x, axis=-1, keepdims=True) + RMS_EPS) * g


def _rope128(x, tab_ref, shift):
    return (x * tab_ref[0] + pltpu.roll(x, LANES - shift, 1) * tab_ref[1]
            + pltpu.roll(x, shift, 1) * tab_ref[2])


def _rope_tables(pos, rot, lane_of):
    half = rot // 2
    inv = jnp.float32(ROPE_THETA) ** (-jnp.arange(half, dtype=F32) * 2.0 / rot)
    ang = pos.astype(F32)[:, None] * inv
    cos, sin = jnp.cos(ang), jnp.sin(ang)
    idx = np.array([lane_of(j) for j in range(LANES)])
    lo = (idx >= 0) & (idx < half)
    hi = idx >= half
    src = np.where(idx >= 0, idx % half, 0)
    c = jnp.where(jnp.asarray(lo | hi), cos[:, src], 1.0)
    sa = jnp.where(jnp.asarray(lo), -sin[:, src], 0.0)
    sb = jnp.where(jnp.asarray(hi), sin[:, src], 0.0)
    return jnp.stack([c, sa, sb]).astype(F32)


def _da_lane(j):
    c = j % DA_QK_DIM
    return c if c < DA_ROT else -1


def _mla_lane(j):
    c = j - MLA_NOPE
    return c if 0 <= c < MLA_ROPE else -1


def _inproj_kernel(x_ref, ng_ref, wm_ref, qg_ref, kvg_ref, wuq_ref, tda_ref, tml_ref,
                   qT_ref, k4_ref, v4_ref, kb_ref, vb_ref, lat_ref, kr_ref, qmT_ref, *, q_scale, qm_scale):
    h = _rms(x_ref[...], ng_ref[...]).astype(BF16)
    proj = jnp.dot(h, wm_ref[...], preferred_element_type=F32)
    o_k, o_v = DA_COLS, 2 * DA_COLS
    o_cq = 3 * DA_COLS
    o_ckv = o_cq + MLA_Q_LORA
    o_kr = o_ckv + MLA_KV_LORA
    for i in range(DA_HEADS):
        lo, hi = i * LANES, (i + 1) * LANES
        q = _rope128(proj[:, lo:hi], tda_ref, DA_ROT // 2) * q_scale
        qT_ref[0, 0, lo:hi, :] = q.T.astype(BF16)
        k = _rope128(proj[:, o_k + lo:o_k + hi], tda_ref, DA_ROT // 2)
        v = proj[:, o_v + lo:o_v + hi]
        k4_ref[0, :, i, :] = k
        v4_ref[0, :, i, :] = v
        kb_ref[:, lo:hi] = k.astype(BF16)
        vb_ref[:, lo:hi] = v.astype(BF16)
    lat_ref[...] = _rms(proj[:, o_ckv:o_ckv + MLA_KV_LORA], kvg_ref[...])
    kr_ref[...] = _rope128(proj[:, o_kr:o_kr + LANES], tml_ref, MLA_ROPE // 2)
    qn = _rms(proj[:, o_cq:o_cq + MLA_Q_LORA], qg_ref[...]).astype(BF16)
    qm = jnp.dot(qn, wuq_ref[...], preferred_element_type=F32)
    for i in range(MLA_HEADS):
        lo, hi = i * LANES, (i + 1) * LANES
        q = _rope128(qm[:, lo:hi], tml_ref, MLA_ROPE // 2) * qm_scale
        qmT_ref[0, 0, lo:hi, :] = q.T.astype(BF16)


def _inproj(x, norm_g, w_main, qg, kvg, wuq, tda, tml, *, tm):
    B, S, D = x.shape
    n = B * S
    nst = S // tm
    wcols = w_main.shape[1]
    const = lambda i: (0, 0)
    tok = lambda i: (i, 0)
    feat_t = lambda i: (i // nst, i % nst, 0, 0)
    tab = lambda i: (0, i % nst, 0)
    return pl.pallas_call(
        functools.partial(_inproj_kernel, q_scale=DA_QK_DIM ** -0.5 * LOG2E,
                          qm_scale=MLA_QK_DIM ** -0.5 * LOG2E),
        grid=(n // tm,),
        in_specs=[
            pl.BlockSpec((tm, D), tok),
            pl.BlockSpec((1, D), const),
            pl.BlockSpec((D, wcols), const),
            pl.BlockSpec((1, MLA_Q_LORA), const),
            pl.BlockSpec((1, MLA_KV_LORA), const),
            pl.BlockSpec((MLA_Q_LORA, MLA_QK_COLS), const),
            pl.BlockSpec((3, tm, LANES), tab),
            pl.BlockSpec((3, tm, LANES), tab),
        ],
        out_specs=[
            pl.BlockSpec((1, 1, DA_COLS, tm), feat_t),
            pl.BlockSpec((1, tm, DA_HEADS, DA_V_DIM), feat_t),
            pl.BlockSpec((1, tm, DA_HEADS, DA_V_DIM), feat_t),
            pl.BlockSpec((tm, DA_COLS), tok),
            pl.BlockSpec((tm, DA_COLS), tok),
            pl.BlockSpec((tm, MLA_KV_LORA), tok),
            pl.BlockSpec((tm, LANES), tok),
            pl.BlockSpec((1, 1, MLA_QK_COLS, tm), feat_t),
        ],
        out_shape=[
            jax.ShapeDtypeStruct((B, nst, DA_COLS, tm), BF16),
            jax.ShapeDtypeStruct((B, S, DA_HEADS, DA_V_DIM), F32),
            jax.ShapeDtypeStruct((B, S, DA_HEADS, DA_V_DIM), F32),
            jax.ShapeDtypeStruct((n, DA_COLS), BF16),
            jax.ShapeDtypeStruct((n, DA_COLS), BF16),
            jax.ShapeDtypeStruct((n, MLA_KV_LORA), F32),
            jax.ShapeDtypeStruct((n, LANES), F32),
            jax.ShapeDtypeStruct((B, nst, MLA_QK_COLS, tm), BF16),
        ],
        compiler_params=pltpu.CompilerParams(
            dimension_semantics=("arbitrary",), vmem_limit_bytes=VMEM_LIMIT_BYTES),
        name="inproj",
    )(x.reshape(n, D), norm_g, w_main, qg, kvg, wuq, tda, tml)


def _n_features(tk):
    return tk // CHUNK + 1


def _key_features(tk, valid_rows, lane0, dtype):
    nf = _n_features(tk)
    row = lax.broadcasted_iota(jnp.int32, (tk, LANES), 0)
    f = lax.broadcasted_iota(jnp.int32, (tk, LANES), 1) - lane0
    chunk_hit = (f >= 0) & (f < nf - 1) & (lax.shift_right_logical(row, 6) == f)
    pad_hit = (f == nf - 1) & (row >= valid_rows)
    return jnp.where(chunk_hit | pad_hit, 1.0, 0.0).astype(dtype)


def _query_coeffs(nrows, ncols, tk, tq, q_rel, diag):
    nf = _n_features(tk)
    r = lax.broadcasted_iota(jnp.int32, (nrows, ncols), 0)
    col = lax.broadcasted_iota(jnp.int32, (nrows, ncols), 1)
    q_chunk = lax.shift_right_logical(q_rel + jnp.where(col >= tq, col - tq, col), 6)
    hidden = (r == nf - 1)
    if diag:
        hidden = hidden | ((r < nf - 1) & (r > q_chunk))
    return jnp.where(hidden, NEG_BIG, 0.0).astype(BF16)


def _expand_kernel(c_ref, kr_ref, wuk_ref, wuv_ref, km_ref, vT_ref, *, valid_rows):
    tk = c_ref.shape[1]
    c = c_ref[0].astype(BF16)
    kn = jnp.dot(c, wuk_ref[...], preferred_element_type=F32)
    tail = kr_ref[0] + _key_features(tk, valid_rows, MLA_QK_DIM, F32)
    for i in range(MLA_HEADS):
        lo, hi = i * LANES, (i + 1) * LANES
        km_ref[0, :, lo:hi] = (kn[:, lo:hi] + tail).astype(BF16)
    vm = jnp.dot(c, wuv_ref[...], preferred_element_type=F32)
    for i in range(MLA_WIDTH // LANES):
        lo, hi = i * LANES, (i + 1) * LANES
        vT_ref[0, 0, lo:hi, :] = vm[:, lo:hi].T.astype(BF16)


def _expand(c, kr, wuk, wuv, *, tk, kv_len):
    B, T, _ = c.shape
    nkv = T // tk
    assert _n_features(tk) <= MLA_FEAT
    const = lambda b, j: (0, 0)
    tile = lambda b, j: (b, j, 0)
    return pl.pallas_call(
        functools.partial(_expand_kernel, valid_rows=kv_len - (nkv - 1) * tk),
        grid=(B, nkv),
        in_specs=[
            pl.BlockSpec((1, tk, MLA_KV_LORA), tile),
            pl.BlockSpec((1, tk, LANES), tile),
            pl.BlockSpec((MLA_KV_LORA, MLA_QK_COLS), const),
            pl.BlockSpec((MLA_KV_LORA, MLA_WIDTH), const),
        ],
        out_specs=[
            pl.BlockSpec((1, tk, MLA_QK_COLS), tile),
            pl.BlockSpec((1, 1, MLA_WIDTH, tk), lambda b, j: (b, j, 0, 0)),
        ],
        out_shape=[
            jax.ShapeDtypeStruct((B, T, MLA_QK_COLS), BF16),
            jax.ShapeDtypeStruct((B, nkv, MLA_WIDTH, tk), BF16),
        ],
        compiler_params=pltpu.CompilerParams(
            dimension_semantics=("arbitrary", "arbitrary"), vmem_limit_bytes=VMEM_LIMIT_BYTES),
        name="kv_expand",
    )(c, kr, wuk, wuv)


def _check_tiling(*, nq, tq, tk, nkv, q_off, kv_len):
    single = nq == 1 and nkv == 1
    causal = q_off == 0 and tq == tk and nq == nkv and kv_len == nkv * tk and (nq * (nq + 1) // 2) % 2 == 0
    assert single or causal, (nq, tq, tk, nkv, q_off, kv_len)


class _Chain(NamedTuple):
    load_k: Callable
    load_vT: Callable
    load_q: Callable
    s: object
    mx: object
    p: object
    a: object
    m: object
    l: object
    lfin: object
    acc: object


def _flash_flat(chains, nq, finalize):
    for c in chains:
        tk = c.s.shape[1]
        c.m[...] = jnp.full(c.m.shape, NEG_BIG, F32)
        c.l[...] = jnp.zeros(c.l.shape, F32)
        c.acc[...] = jnp.zeros(c.acc.shape, F32)

    def advance(u):
        qi, j = u
        last = j == qi
        return jnp.where(last, qi + 1, qi), jnp.where(last, 0, j + 1)

    def scores(u, slot):
        qi, j = u
        for c in chains:
            s = jnp.dot(c.load_k(j), c.load_q(qi, j == qi), preferred_element_type=F32)
            c.s[slot] = s
            c.mx[slot] = jnp.max(s, axis=0, keepdims=True)

    def softmax(u, slot):
        qi, j = u
        par = lax.rem(qi, 2)
        for c in chains:
            ncols = c.m.shape[1]
            m_prev = jnp.where(j == 0, NEG_BIG, c.m[...])
            m_new = jnp.maximum(m_prev, c.mx[slot])
            alpha = jnp.exp2(m_prev - m_new)
            c.m[...] = m_new
            c.a[slot] = alpha
            part = jnp.zeros((SUM_ROWS, ncols), F32)
            for r in range(0, tk, BF16_ROWS):
                p = jnp.exp2(c.s[slot, r:r + BF16_ROWS, :] - m_new)
                c.p[slot, r:r + BF16_ROWS, :] = p.astype(BF16)
                for h in range(0, BF16_ROWS, SUM_ROWS):
                    part = part + p[h:h + SUM_ROWS]
            l_new = alpha * c.l[...] + jnp.sum(part, axis=0, keepdims=True)
            c.l[...] = l_new
            c.lfin[par] = l_new

    def values(u, slot):
        qi, j = u
        par = lax.rem(qi, 2)
        for c in chains:
            pv = jnp.dot(c.load_vT(j), c.p[slot], preferred_element_type=F32)
            c.acc[par] = c.a[slot] * c.acc[par] + pv

    zero = jnp.int32(0)
    u0 = (zero, zero)
    n_units = nq * (nq + 1) // 2
    if n_units == 1:
        scores(u0, 0)
        softmax(u0, 0)
        values(u0, 0)
        finalize(0)
        return

    scores(u0, 0)
    scores(advance(u0), 1)
    softmax(u0, 0)

    def pair(t, ua):
        ub = advance(ua)
        uc = advance(ub)
        ud = advance(uc)
        scores(uc, 0)
        softmax(ub, 1)
        values(ua, 0)
        scores(ud, 1)
        softmax(uc, 0)
        values(ub, 1)
        done_a = ua[1] == ua[0]
        done_b = ub[1] == ub[0]

        @pl.when(jnp.logical_or(done_a, done_b))
        def _():
            finalize(jnp.where(done_a, ua[0], ub[0]))

        return uc

    ua = lax.fori_loop(0, n_units // 2 - 1, pair, u0)
    ub = advance(ua)
    softmax(ub, 1)
    values(ua, 0)
    values(ub, 1)
    finalize(nq - 1)


def _da_attn_kernel(lam_ref, hg_ref, qT_ref, k_ref, v_ref, o_ref,
                    vT_ref, qs_ref, feat_ref, gv_ref, s_ref, mx_ref, p_ref, a_ref, m_ref, l_ref, lfin_ref,
                    acc_ref, *, nq, tq, tk, nkv, q_off, kv_len, lam_init):
    valid_rows = kv_len - (nkv - 1) * tk
    feat_ref[...] = _key_features(tk, valid_rows, 0, BF16)
    q_rel = q_off if nkv == 1 else 0
    gv_ref[0] = _query_coeffs(LANES, 2 * tq, tk, tq, q_rel, diag=False)
    gv_ref[1] = _query_coeffs(LANES, 2 * tq, tk, tq, q_rel, diag=True)

    row = lax.broadcasted_iota(jnp.int32, (LANES, tq), 0)
    zero = jnp.zeros((LANES, tq), BF16)
    chains = []
    for hh in range(HEAD_PAIR):
        cols = slice(hh * LANES, (hh + 1) * LANES)
        for c in range(nkv):
            vT_ref[hh, c] = v_ref[0, c * tk:(c + 1) * tk, cols].astype(F32).T.astype(BF16)
        for qi in range(nq):
            qT = qT_ref[0, qi, cols, :]
            qs_ref[hh, qi, :, :tq] = jnp.where(row < DA_QK_DIM, qT, zero)
            qs_ref[hh, qi, :, tq:] = jnp.where(row >= DA_QK_DIM, qT, zero)
        chains.append(_Chain(
            load_k=lambda j, cols=cols: jnp.concatenate(
                [k_ref[0, pl.ds(pl.multiple_of(j * tk, tk), tk), cols], feat_ref[...]], axis=1),
            load_vT=lambda j, hh=hh: vT_ref[hh, j],
            load_q=lambda qi, diag, hh=hh: jnp.concatenate(
                [qs_ref[hh, qi], gv_ref[diag.astype(jnp.int32)]], axis=0),
            s=s_ref.at[hh], mx=mx_ref.at[hh], p=p_ref.at[hh], a=a_ref.at[hh], m=m_ref.at[hh],
            l=l_ref.at[hh], lfin=lfin_ref.at[hh], acc=acc_ref.at[hh]))

    lf = lam_ref[...]
    lam = (jnp.exp(jnp.sum(lf[0:1] * lf[1:2], axis=-1, keepdims=True))
           - jnp.exp(jnp.sum(lf[2:3] * lf[3:4], axis=-1, keepdims=True)) + lam_init)

    def finalize(qi):
        par = qi % 2
        row0 = qi * tq if isinstance(qi, int) else pl.multiple_of(qi * tq, tq)
        for hh in range(HEAD_PAIR):
            o = acc_ref[hh, par] * (1.0 / lfin_ref[hh, par])
            o = (o[:, :tq] - lam * o[:, tq:]).T
            o_ref[0, pl.ds(row0, tq), hh * LANES:(hh + 1) * LANES] = (
                _rms(o, hg_ref[...]) * (1.0 - lam_init)).astype(BF16)

    _flash_flat(chains, nq, finalize)


def _da_attention(lam, hg, qT, k, v, *, tk, q_off, kv_len, lam_init):
    B, nq, _, tq = qT.shape
    T = k.shape[1]
    nkv = T // tk
    _check_tiling(nq=nq, tq=tq, tk=tk, nkv=nkv, q_off=q_off, kv_len=kv_len)
    assert _n_features(tk) <= LANES
    pair = HEAD_PAIR * LANES
    head_kv = lambda b, h: (b, 0, h)
    return pl.pallas_call(
        functools.partial(_da_attn_kernel, nq=nq, tq=tq, tk=tk, nkv=nkv, q_off=q_off, kv_len=kv_len,
                          lam_init=lam_init),
        grid=(B, DA_HEADS // HEAD_PAIR),
        in_specs=[
            pl.BlockSpec(lam.shape, lambda b, h: (0, 0)),
            pl.BlockSpec((1, DA_V_DIM), lambda b, h: (0, 0)),
            pl.BlockSpec((1, nq, pair, tq), lambda b, h: (b, 0, h, 0)),
            pl.BlockSpec((1, T, pair), head_kv),
            pl.BlockSpec((1, T, pair), head_kv),
        ],
        out_specs=pl.BlockSpec((1, nq * tq, pair), head_kv),
        out_shape=jax.ShapeDtypeStruct((B, nq * tq, DA_COLS), BF16),
        scratch_shapes=[
            pltpu.VMEM((HEAD_PAIR, nkv, DA_V_DIM, tk), BF16),
            pltpu.VMEM((HEAD_PAIR, nq, LANES, 2 * tq), BF16),
            pltpu.VMEM((tk, LANES), BF16),
            pltpu.VMEM((2, LANES, 2 * tq), BF16),
            pltpu.VMEM((HEAD_PAIR, 2, tk, 2 * tq), F32),
            pltpu.VMEM((HEAD_PAIR, 2, 1, 2 * tq), F32),
            pltpu.VMEM((HEAD_PAIR, 2, tk, 2 * tq), BF16),
            pltpu.VMEM((HEAD_PAIR, 2, 1, 2 * tq), F32),
            pltpu.VMEM((HEAD_PAIR, 1, 2 * tq), F32),
            pltpu.VMEM((HEAD_PAIR, 1, 2 * tq), F32),
            pltpu.VMEM((HEAD_PAIR, 2, 1, 2 * tq), F32),
            pltpu.VMEM((HEAD_PAIR, 2, DA_V_DIM, 2 * tq), F32),
        ],
        compiler_params=pltpu.CompilerParams(
            dimension_semantics=("arbitrary", "arbitrary"), vmem_limit_bytes=VMEM_LIMIT_BYTES),
        name="da_attention",
    )(lam, hg, qT, k, v)


def _mla_attn_kernel(qT_ref, k_ref, vT_ref, o_ref, gv_ref, s_ref, mx_ref, p_ref, a_ref, m_ref, l_ref,
                     lfin_ref, acc_ref, *, nq, tq, tk, nkv, q_off):
    q_rel = q_off if nkv == 1 else 0
    gv_ref[0] = _query_coeffs(MLA_FEAT, tq, tk, tq, q_rel, diag=False)
    gv_ref[1] = _query_coeffs(MLA_FEAT, tq, tk, tq, q_rel, diag=True)
    chains = []
    for hh in range(HEAD_PAIR):
        rows = slice(hh * MLA_V_DIM, (hh + 1) * MLA_V_DIM)
        cols = slice(hh * LANES, (hh + 1) * LANES)
        qrows = slice(hh * LANES, hh * LANES + MLA_QK_DIM)
        chains.append(_Chain(
            load_k=lambda j, cols=cols: k_ref[0, pl.ds(pl.multiple_of(j * tk, tk), tk), cols],
            load_vT=lambda j, rows=rows: vT_ref[0, j, rows, :],
            load_q=lambda qi, diag, qrows=qrows: jnp.concatenate(
                [qT_ref[0, qi, qrows, :], gv_ref[diag.astype(jnp.int32)]], axis=0),
            s=s_ref.at[hh], mx=mx_ref.at[hh], p=p_ref.at[hh], a=a_ref.at[hh], m=m_ref.at[hh],
            l=l_ref.at[hh], lfin=lfin_ref.at[hh], acc=acc_ref.at[:, rows]))

    def finalize(qi):
        par = qi % 2
        row0 = qi * tq if isinstance(qi, int) else pl.multiple_of(qi * tq, tq)
        o = acc_ref[par]
        halves = [o[hh * MLA_V_DIM:(hh + 1) * MLA_V_DIM] * (1.0 / lfin_ref[hh, par])
                  for hh in range(HEAD_PAIR)]
        o_ref[0, pl.ds(row0, tq), :] = jnp.concatenate(halves, axis=0).T.astype(BF16)

    _flash_flat(chains, nq, finalize)


def _mla_attention(qmT, km, vT, *, q_off, kv_len):
    B, nq, _, tq = qmT.shape
    nkv, tk = vT.shape[1], vT.shape[3]
    T = km.shape[1]
    _check_tiling(nq=nq, tq=tq, tk=tk, nkv=nkv, q_off=q_off, kv_len=kv_len)
    return pl.pallas_call(
        functools.partial(_mla_attn_kernel, nq=nq, tq=tq, tk=tk, nkv=nkv, q_off=q_off),
        grid=(B, MLA_HEADS // HEAD_PAIR),
        in_specs=[
            pl.BlockSpec((1, nq, HEAD_PAIR * LANES, tq), lambda b, h: (b, 0, h, 0)),
            pl.BlockSpec((1, T, HEAD_PAIR * LANES), lambda b, h: (b, 0, h)),
            pl.BlockSpec((1, nkv, HEAD_PAIR * MLA_V_DIM, tk), lambda b, h: (b, 0, h, 0)),
        ],
        out_specs=pl.BlockSpec((1, nq * tq, HEAD_PAIR * MLA_V_DIM), lambda b, h: (b, 0, h)),
        out_shape=jax.ShapeDtypeStruct((B, nq * tq, MLA_WIDTH), BF16),
        scratch_shapes=[
            pltpu.VMEM((2, MLA_FEAT, tq), BF16),
            pltpu.VMEM((HEAD_PAIR, 2, tk, tq), F32),
            pltpu.VMEM((HEAD_PAIR, 2, 1, tq), F32),
            pltpu.VMEM((HEAD_PAIR, 2, tk, tq), BF16),
            pltpu.VMEM((HEAD_PAIR, 2, 1, tq), F32),
            pltpu.VMEM((HEAD_PAIR, 1, tq), F32),
            pltpu.VMEM((HEAD_PAIR, 1, tq), F32),
            pltpu.VMEM((HEAD_PAIR, 2, 1, tq), F32),
            pltpu.VMEM((2, HEAD_PAIR * MLA_V_DIM, tq), F32),
        ],
        compiler_params=pltpu.CompilerParams(
            dimension_semantics=("arbitrary", "arbitrary"), vmem_limit_bytes=VMEM_LIMIT_BYTES),
        name="mla_attention",
    )(qmT, km, vT)


def _epilogue_kernel(x_ref, oa_ref, ob_ref, ng_ref, wzg_ref, gb_ref, wa_ref, wb_ref, wo_ref, fg_ref,
                     y_ref, *, final_norm):
    x = x_ref[...]
    d = x.shape[1]
    h = _rms(x, ng_ref[...]).astype(BF16)
    zg = jnp.dot(h, wzg_ref[...], preferred_element_type=F32)
    ga = (oa_ref[...].astype(F32) * jax.nn.silu(zg[:, :DA_COLS])).astype(BF16)
    gb = (ob_ref[...].astype(F32) * jax.nn.silu(zg[:, DA_COLS:DA_COLS + MLA_WIDTH])).astype(BF16)
    ya = jnp.dot(ga, wa_ref[...], preferred_element_type=F32)
    yb = jnp.dot(gb, wb_ref[...], preferred_element_type=F32)
    g = jax.nn.sigmoid(zg[:, DA_COLS + MLA_WIDTH:] + gb_ref[...])
    m = (g[:, :d] * ya + g[:, d:] * yb).astype(BF16)
    out = x + jnp.dot(m, wo_ref[...], preferred_element_type=F32)
    y_ref[...] = _rms(out, fg_ref[...]) if final_norm else out


def _epilogue(x, oa, ob, norm_g, wzg, gate_b, wa, wb, wo, final_g, *, tm, final_norm):
    n, D = x.shape
    const = lambda i: (0, 0)
    tok = lambda i: (i, 0)
    return pl.pallas_call(
        functools.partial(_epilogue_kernel, final_norm=final_norm),
        grid=(n // tm,),
        in_specs=[
            pl.BlockSpec((tm, D), tok),
            pl.BlockSpec((tm, DA_COLS), tok),
            pl.BlockSpec((tm, MLA_WIDTH), tok),
            pl.BlockSpec((1, D), const),
            pl.BlockSpec(wzg.shape, const),
            pl.BlockSpec((1, 2 * D), const),
            pl.BlockSpec(wa.shape, const),
            pl.BlockSpec(wb.shape, const),
            pl.BlockSpec(wo.shape, const),
            pl.BlockSpec((1, D), const),
        ],
        out_specs=pl.BlockSpec((tm, D), tok),
        out_shape=jax.ShapeDtypeStruct((n, D), F32),
        compiler_params=pltpu.CompilerParams(
            dimension_semantics=("arbitrary",), vmem_limit_bytes=VMEM_LIMIT_BYTES),
        name="epilogue",
    )(x, oa, ob, norm_g, wzg, gate_b, wa, wb, wo, final_g)


def _tiles(S):
    t = min(512, S)
    assert S % t == 0 and t % LANES == 0
    return t


def _prep_weights(w_in, mla_w_uq, mla_w_uk):
    D = w_in.shape[0]
    sizes = (DA_COLS, DA_COLS, DA_COLS, DA_COLS, MLA_Q_LORA, MLA_KV_LORA, MLA_ROPE, MLA_WIDTH, 2 * D)
    assert w_in.shape[1] == sum(sizes)
    w_q, w_k, w_v, w_za, w_cq, w_ckv, w_kr, w_zb, w_g = jnp.split(w_in, np.cumsum(sizes)[:-1].tolist(), axis=1)
    w_kr = jnp.pad(w_kr, ((0, 0), (MLA_NOPE, LANES - MLA_QK_DIM)))
    w_main = jnp.concatenate([w_q, w_k, w_v, w_cq, w_ckv, w_kr], axis=1).astype(BF16)
    w_zg = jnp.concatenate([w_za, w_zb, w_g], axis=1).astype(BF16)
    pad_heads = lambda w, dh: jnp.pad(
        w.reshape(w.shape[0], MLA_HEADS, dh), ((0, 0), (0, 0), (0, LANES - dh))
    ).reshape(w.shape[0], MLA_QK_COLS).astype(BF16)
    return w_main, w_zg, pad_heads(mla_w_uq, MLA_QK_DIM), pad_heads(mla_w_uk, MLA_NOPE)


def _layer(x, past, lam_init, final_norm, norm_g, w_in, gate_b, da_lambda, da_head_norm_g, mla_q_norm_g,
           mla_kv_norm_g, mla_w_uq, mla_w_uk, mla_w_uv, w_branch_a, w_branch_b, w_out, final_norm_g):
    B, S, D = x.shape
    past_len = 0 if past is None else past[0].shape[1]
    Sp = -(-S // LANES) * LANES
    xp = jnp.pad(x, ((0, 0), (0, Sp - S), (0, 0))) if Sp != S else x
    tm = _tiles(Sp)
    pos = past_len + jnp.arange(Sp, dtype=jnp.int32)
    tda = _rope_tables(pos, DA_ROT, _da_lane)
    tml = _rope_tables(pos, MLA_ROPE, _mla_lane)
    w_main, w_zg, wuq, wuk = _prep_weights(w_in, mla_w_uq, mla_w_uk)
    row = lambda v: v.reshape(1, -1)

    qT, k_new, v_new, k_b, v_b, lat_new, kr_new, qmT = _inproj(
        xp, row(norm_g), w_main, row(mla_q_norm_g), row(mla_kv_norm_g), wuq, tda, tml, tm=tm)
    lat_new = lat_new.reshape(B, Sp, MLA_KV_LORA)
    kr_new = kr_new.reshape(B, Sp, LANES)

    kv_len = past_len + S
    if past is None:
        k_all, v_all = k_b.reshape(B, Sp, DA_COLS), v_b.reshape(B, Sp, DA_COLS)
        c_all, kr_all = lat_new, kr_new
        tk = tm
    else:
        pk, pv, pc, pr = past
        tk = -(-kv_len // LANES) * LANES
        cat = lambda old, new: jnp.pad(jnp.concatenate([old, new[:, :S]], axis=1),
                                       ((0, 0), (0, tk - kv_len), (0, 0)))
        k_all = cat(pk.reshape(B, past_len, DA_COLS), k_new.reshape(B, Sp, DA_COLS)).astype(BF16)
        v_all = cat(pv.reshape(B, past_len, DA_COLS), v_new.reshape(B, Sp, DA_COLS)).astype(BF16)
        c_all = cat(pc, lat_new)
        kr_all = cat(jnp.pad(pr, ((0, 0), (0, 0), (MLA_NOPE, LANES - MLA_QK_DIM))), kr_new)

    o_a = _da_attention(da_lambda, row(da_head_norm_g), qT, k_all, v_all,
                        tk=tk, q_off=past_len, kv_len=kv_len, lam_init=lam_init)
    km, vmT = _expand(c_all, kr_all, wuk, mla_w_uv.astype(BF16), tk=tk, kv_len=kv_len)
    o_b = _mla_attention(qmT, km, vmT, q_off=past_len, kv_len=kv_len)

    n = B * Sp
    y = _epilogue(xp.reshape(n, D), o_a.reshape(n, DA_COLS), o_b.reshape(n, MLA_WIDTH), row(norm_g), w_zg,
                  row(gate_b), w_branch_a.astype(BF16), w_branch_b.astype(BF16), w_out.astype(BF16),
                  row(final_norm_g), tm=_tiles(n), final_norm=final_norm)
    y = y.reshape(B, Sp, D)[:, :S]
    new = (k_new[:, :S], v_new[:, :S], lat_new[:, :S], kr_new[:, :S, MLA_NOPE:MLA_QK_DIM])
    return y, new


def kernel(x_prompt, x_sample, cache_da_k, cache_da_v, cache_mla_latent, cache_mla_krope, norm_g, w_in, gate_b, da_lambda, da_head_norm_g, mla_q_norm_g, mla_kv_norm_g, mla_w_uq, mla_w_uk, mla_w_uv, w_branch_a, w_branch_b, w_out, final_norm_g):
    depth = w_in.shape[0]
    hp, hs = x_prompt, x_sample
    rows_p, rows_s = [], []
    for l in range(depth):
        lam_init = 0.8 - 0.6 * math.exp(-0.3 * l)
        last = l == depth - 1
        w = (norm_g[l], w_in[l], gate_b[l], da_lambda[l], da_head_norm_g[l], mla_q_norm_g[l], mla_kv_norm_g[l],
             mla_w_uq[l], mla_w_uk[l], mla_w_uv[l], w_branch_a[l], w_branch_b[l], w_out[l], final_norm_g)
        hp, new_p = _layer(hp, None, lam_init, last, *w)
        past = (cache_da_k[l], cache_da_v[l], cache_mla_latent[l], cache_mla_krope[l])
        hs, new_s = _layer(hs, past, lam_init, last, *w)
        rows_p.append(new_p)
        rows_s.append(new_s)
    stack = lambda rows, i: jnp.stack([r[i] for r in rows], 0)
    return (hp, hs, stack(rows_p, 0), stack(rows_p, 1), stack(rows_p, 2), stack(rows_p, 3),
            stack(rows_s, 0), stack(rows_s, 1), stack(rows_s, 2), stack(rows_s, 3))
```

```python
import functools
import math
from typing import Callable, NamedTuple

import jax
import jax.numpy as jnp
import numpy as np
from jax import lax
from jax.experimental import pallas as pl
from jax.experimental.pallas import tpu as pltpu

CHUNK = 64
ROPE_THETA = 500000.0
RMS_EPS = 1e-6
DA_HEADS = 4
DA_QK_DIM = 64
DA_V_DIM = 2 * DA_QK_DIM
DA_ROT = DA_QK_DIM // 4
DA_COLS = DA_HEADS * DA_V_DIM
MLA_HEADS = 8
MLA_Q_LORA = 384
MLA_KV_LORA = 256
MLA_NOPE = 64
MLA_ROPE = 32
MLA_V_DIM = 64
MLA_WIDTH = MLA_HEADS * MLA_V_DIM
MLA_QK_DIM = MLA_NOPE + MLA_ROPE

LANES = 128
SUM_ROWS = 8
BF16_ROWS = 16
SUM_PAD = BF16_ROWS
MLA_VT_ROWS = 64 + SUM_PAD
MLA_QK_COLS = MLA_HEADS * LANES
MLA_FEAT = LANES - MLA_QK_DIM
LOG2E = math.log2(math.e)
NEG_BIG = -1e30
VMEM_LIMIT_BYTES = 56 * 1024 * 1024
HEAD_PAIR = 2
LOOP_STAGES = 2

F32 = jnp.float32
BF16 = jnp.bfloat16


def _rms(x, g):
    return x * lax.rsqrt(jnp.mean(x * x, axis=-1, keepdims=True) + RMS_EPS) * g


def _rope128(x, tab_ref, shift):
    return (x * tab_ref[0] + pltpu.roll(x, LANES - shift, 1) * tab_ref[1]
            + pltpu.roll(x, shift, 1) * tab_ref[2])


def _rope_tables(pos, rot, lane_of):
    half = rot // 2
    inv = jnp.float32(ROPE_THETA) ** (-jnp.arange(half, dtype=F32) * 2.0 / rot)
    ang = pos.astype(F32)[:, None] * inv
    cos, sin = jnp.cos(ang), jnp.sin(ang)
    idx = np.array([lane_of(j) for j in range(LANES)])
    lo = (idx >= 0) & (idx < half)
    hi = idx >= half
    src = np.where(idx >= 0, idx % half, 0)
    c = jnp.where(jnp.asarray(lo | hi), cos[:, src], 1.0)
    sa = jnp.where(jnp.asarray(lo), -sin[:, src], 0.0)
    sb = jnp.where(jnp.asarray(hi), sin[:, src], 0.0)
    return jnp.stack([c, sa, sb]).astype(F32)


def _da_lane(j):
    c = j % DA_QK_DIM
    return c if c < DA_ROT else -1


def _mla_lane(j):
    c = j - MLA_NOPE
    return c if 0 <= c < MLA_ROPE else -1


def _inproj_kernel(x_ref, ng_ref, wm_ref, qg_ref, kvg_ref, wuq_ref, tda_ref, tml_ref,
                   qT_ref, k4_ref, v4_ref, kb_ref, vb_ref, lat_ref, kr_ref, qmT_ref, *, q_scale, qm_scale):
    h = _rms(x_ref[...], ng_ref[...]).astype(BF16)
    proj = jnp.dot(h, wm_ref[...], preferred_element_type=F32)
    o_k, o_v = DA_COLS, 2 * DA_COLS
    o_cq = 3 * DA_COLS
    o_ckv = o_cq + MLA_Q_LORA
    o_kr = o_ckv + MLA_KV_LORA
    for i in range(DA_HEADS):
        lo, hi = i * LANES, (i + 1) * LANES
        q = _rope128(proj[:, lo:hi], tda_ref, DA_ROT // 2) * q_scale
        qT_ref[0, 0, lo:hi, :] = q.T.astype(BF16)
        k = _rope128(proj[:, o_k + lo:o_k + hi], tda_ref, DA_ROT // 2)
        v = proj[:, o_v + lo:o_v + hi]
        k4_ref[0, :, i, :] = k
        v4_ref[0, :, i, :] = v
        kb_ref[:, lo:hi] = k.astype(BF16)
        vb_ref[:, lo:hi] = v.astype(BF16)
    lat_ref[...] = _rms(proj[:, o_ckv:o_ckv + MLA_KV_LORA], kvg_ref[...])
    kr_ref[...] = _rope128(proj[:, o_kr:o_kr + LANES], tml_ref, MLA_ROPE // 2)
    qn = _rms(proj[:, o_cq:o_cq + MLA_Q_LORA], qg_ref[...]).astype(BF16)
    qm = jnp.dot(qn, wuq_ref[...], preferred_element_type=F32)
    for i in range(MLA_HEADS):
        lo, hi = i * LANES, (i + 1) * LANES
        q = _rope128(qm[:, lo:hi], tml_ref, MLA_ROPE // 2) * qm_scale
        qmT_ref[0, 0, lo:hi, :] = q.T.astype(BF16)


def _inproj(x, norm_g, w_main, qg, kvg, wuq, tda, tml, *, tm):
    B, S, D = x.shape
    n = B * S
    nst = S // tm
    wcols = w_main.shape[1]
    const = lambda i: (0, 0)
    tok = lambda i: (i, 0)
    feat_t = lambda i: (i // nst, i % nst, 0, 0)
    tab = lambda i: (0, i % nst, 0)
    return pl.pallas_call(
        functools.partial(_inproj_kernel, q_scale=DA_QK_DIM ** -0.5 * LOG2E,
                          qm_scale=MLA_QK_DIM ** -0.5 * LOG2E),
        grid=(n // tm,),
        in_specs=[
            pl.BlockSpec((tm, D), tok),
            pl.BlockSpec((1, D), const),
            pl.BlockSpec((D, wcols), const),
            pl.BlockSpec((1, MLA_Q_LORA), const),
            pl.BlockSpec((1, MLA_KV_LORA), const),
            pl.BlockSpec((MLA_Q_LORA, MLA_QK_COLS), const),
            pl.BlockSpec((3, tm, LANES), tab),
            pl.BlockSpec((3, tm, LANES), tab),
        ],
        out_specs=[
            pl.BlockSpec((1, 1, DA_COLS, tm), feat_t),
            pl.BlockSpec((1, tm, DA_HEADS, DA_V_DIM), feat_t),
            pl.BlockSpec((1, tm, DA_HEADS, DA_V_DIM), feat_t),
            pl.BlockSpec((tm, DA_COLS), tok),
            pl.BlockSpec((tm, DA_COLS), tok),
            pl.BlockSpec((tm, MLA_KV_LORA), tok),
            pl.BlockSpec((tm, LANES), tok),
            pl.BlockSpec((1, 1, MLA_QK_COLS, tm), feat_t),
        ],
        out_shape=[
            jax.ShapeDtypeStruct((B, nst, DA_COLS, tm), BF16),
            jax.ShapeDtypeStruct((B, S, DA_HEADS, DA_V_DIM), F32),
            jax.ShapeDtypeStruct((B, S, DA_HEADS, DA_V_DIM), F32),
            jax.ShapeDtypeStruct((n, DA_COLS), BF16),
            jax.ShapeDtypeStruct((n, DA_COLS), BF16),
            jax.ShapeDtypeStruct((n, MLA_KV_LORA), F32),
            jax.ShapeDtypeStruct((n, LANES), F32),
            jax.ShapeDtypeStruct((B, nst, MLA_QK_COLS, tm), BF16),
        ],
        compiler_params=pltpu.CompilerParams(
            dimension_semantics=("arbitrary",), vmem_limit_bytes=VMEM_LIMIT_BYTES),
        name="inproj",
    )(x.reshape(n, D), norm_g, w_main, qg, kvg, wuq, tda, tml)


def _n_features(tk):
    return tk // CHUNK + 1


def _key_features(tk, valid_rows, lane0, dtype):
    nf = _n_features(tk)
    row = lax.broadcasted_iota(jnp.int32, (tk, LANES), 0)
    f = lax.broadcasted_iota(jnp.int32, (tk, LANES), 1) - lane0
    chunk_hit = (f >= 0) & (f < nf - 1) & (lax.shift_right_logical(row, 6) == f)
    pad_hit = (f == nf - 1) & (row >= valid_rows)
    return jnp.where(chunk_hit | pad_hit, 1.0, 0.0).astype(dtype)


def _query_coeffs(nrows, ncols, tk, tq, q_rel, diag):
    nf = _n_features(tk)
    r = lax.broadcasted_iota(jnp.int32, (nrows, ncols), 0)
    col = lax.broadcasted_iota(jnp.int32, (nrows, ncols), 1)
    q_chunk = lax.shift_right_logical(q_rel + jnp.where(col >= tq, col - tq, col), 6)
    hidden = (r == nf - 1)
    if diag:
        hidden = hidden | ((r < nf - 1) & (r > q_chunk))
    return jnp.where(hidden, NEG_BIG, 0.0).astype(BF16)


def _expand_kernel(c_ref, kr_ref, wuk_ref, wuv_ref, km_ref, vT_ref, *, valid_rows):
    tk = c_ref.shape[1]
    c = c_ref[0].astype(BF16)
    kn = jnp.dot(c, wuk_ref[...], preferred_element_type=F32)
    tail = kr_ref[0] + _key_features(tk, valid_rows, MLA_QK_DIM, F32)
    for i in range(MLA_HEADS):
        lo, hi = i * LANES, (i + 1) * LANES
        km_ref[0, :, lo:hi] = (kn[:, lo:hi] + tail).astype(BF16)
    vm = jnp.dot(c, wuv_ref[...], preferred_element_type=F32)
    ones = _ones_rows(tk)
    for i in range(MLA_WIDTH // LANES):
        vt = vm[:, i * LANES:(i + 1) * LANES].T.astype(BF16)
        for hh in range(LANES // MLA_V_DIM):
            r0 = (i * (LANES // MLA_V_DIM) + hh) * MLA_VT_ROWS
            vT_ref[0, 0, r0:r0 + MLA_V_DIM, :] = vt[hh * MLA_V_DIM:(hh + 1) * MLA_V_DIM]
            vT_ref[0, 0, r0 + MLA_V_DIM:r0 + MLA_VT_ROWS, :] = ones


def _expand(c, kr, wuk, wuv, *, tk, kv_len):
    B, T, _ = c.shape
    nkv = T // tk
    assert _n_features(tk) <= MLA_FEAT
    const = lambda b, j: (0, 0)
    tile = lambda b, j: (b, j, 0)
    return pl.pallas_call(
        functools.partial(_expand_kernel, valid_rows=kv_len - (nkv - 1) * tk),
        grid=(B, nkv),
        in_specs=[
            pl.BlockSpec((1, tk, MLA_KV_LORA), tile),
            pl.BlockSpec((1, tk, LANES), tile),
            pl.BlockSpec((MLA_KV_LORA, MLA_QK_COLS), const),
            pl.BlockSpec((MLA_KV_LORA, MLA_WIDTH), const),
        ],
        out_specs=[
            pl.BlockSpec((1, tk, MLA_QK_COLS), tile),
            pl.BlockSpec((1, 1, MLA_HEADS * MLA_VT_ROWS, tk), lambda b, j: (b, j, 0, 0)),
        ],
        out_shape=[
            jax.ShapeDtypeStruct((B, T, MLA_QK_COLS), BF16),
            jax.ShapeDtypeStruct((B, nkv, MLA_HEADS * MLA_VT_ROWS, tk), BF16),
        ],
        compiler_params=pltpu.CompilerParams(
            dimension_semantics=("arbitrary", "arbitrary"), vmem_limit_bytes=VMEM_LIMIT_BYTES),
        name="kv_expand",
    )(c, kr, wuk, wuv)


def _check_tiling(*, nq, tq, tk, nkv, q_off, kv_len):
    single = nq == 1 and nkv == 1
    causal = q_off == 0 and tq == tk and nq == nkv and kv_len == nkv * tk and (nq * (nq + 1) // 2) % 2 == 0
    assert single or causal, (nq, tq, tk, nkv, q_off, kv_len)


class _Chain(NamedTuple):
    load_k: Callable
    load_vT: Callable
    load_q: Callable
    s: object
    mx: object
    p: object
    a: object
    m: object
    acc: object


def _flash_flat(chains, nq, finalize):
    for c in chains:
        tk = c.s.shape[1]
        c.m[...] = jnp.full(c.m.shape, NEG_BIG, F32)
        c.acc[...] = jnp.zeros(c.acc.shape, F32)

    def advance(u):
        qi, j = u
        last = j == qi
        return jnp.where(last, qi + 1, qi), jnp.where(last, 0, j + 1)

    def scores(u, slot):
        qi, j = u
        for c in chains:
            s = jnp.dot(c.load_k(j, 0, tk), c.load_q(qi, j == qi), preferred_element_type=F32)
            c.s[slot] = s
            c.mx[slot] = jnp.max(s, axis=0, keepdims=True)

    def softmax(u, slot):
        qi, j = u
        for c in chains:
            m_prev = jnp.where(j == 0, NEG_BIG, c.m[...])
            m_new = jnp.maximum(m_prev, c.mx[slot])
            c.m[...] = m_new
            c.a[slot] = jnp.exp2(m_prev - m_new)
            for r in range(0, tk, BF16_ROWS):
                x = (c.s[slot, r:r + BF16_ROWS, :] - m_new).astype(BF16)
                c.p[slot, r:r + BF16_ROWS, :] = jnp.exp2(x)

    def values(u, slot):
        qi, j = u
        par = lax.rem(qi, 2)
        for c in chains:
            pv = jnp.dot(c.load_vT(j), c.p[slot], preferred_element_type=F32)
            c.acc[par] = c.a[slot] * c.acc[par] + pv

    def stages(ua, count):
        units = [ua]
        for _ in range(count + 1):
            units.append(advance(units[-1]))
        for i in range(count):
            cur, nxt = (i + 1) % 2, i % 2
            scores(units[i + 2], nxt)
            softmax(units[i + 1], cur)
            values(units[i], nxt)
        for i in range(0, count, 2):
            done_a = units[i][1] == units[i][0]
            done_b = units[i + 1][1] == units[i + 1][0]

            @pl.when(jnp.logical_or(done_a, done_b))
            def _():
                finalize(jnp.where(done_a, units[i][0], units[i + 1][0]))

        return units[count]

    zero = jnp.int32(0)
    u0 = (zero, zero)
    n_units = nq * (nq + 1) // 2
    if n_units == 1:
        scores(u0, 0)
        softmax(u0, 0)
        values(u0, 0)
        finalize(0)
        return

    scores(u0, 0)
    scores(advance(u0), 1)
    softmax(u0, 0)

    ua = lax.fori_loop(0, (n_units - 2) // LOOP_STAGES, lambda t, ua: stages(ua, LOOP_STAGES), u0)
    rest = (n_units - 2) % LOOP_STAGES
    if rest:
        ua = stages(ua, rest)
    ub = advance(ua)
    softmax(ub, 1)
    values(ua, 0)
    values(ub, 1)
    finalize(nq - 1)


def _ones_rows(tk):
    row = lax.broadcasted_iota(jnp.int32, (SUM_PAD, tk), 0)
    return jnp.where(row == 0, 1.0, 0.0).astype(BF16)


def _da_attn_kernel(lam_ref, hg_ref, qT_ref, k_ref, v_ref, o_ref,
                    vT_ref, qs_ref, feat_ref, gv_ref, s_ref, mx_ref, p_ref, a_ref, m_ref, acc_ref,
                    *, nq, tq, tk, nkv, q_off, kv_len, lam_init):
    valid_rows = kv_len - (nkv - 1) * tk
    feat_ref[...] = _key_features(tk, valid_rows, 0, BF16)
    q_rel = q_off if nkv == 1 else 0
    gv_ref[0] = _query_coeffs(LANES, 2 * tq, tk, tq, q_rel, diag=False)
    gv_ref[1] = _query_coeffs(LANES, 2 * tq, tk, tq, q_rel, diag=True)

    row = lax.broadcasted_iota(jnp.int32, (LANES, tq), 0)
    zero = jnp.zeros((LANES, tq), BF16)
    chains = []
    for hh in range(HEAD_PAIR):
        cols = slice(hh * LANES, (hh + 1) * LANES)
        for c in range(nkv):
            vT_ref[hh, c, :DA_V_DIM] = v_ref[0, c * tk:(c + 1) * tk, cols].astype(F32).T.astype(BF16)
            vT_ref[hh, c, DA_V_DIM:] = _ones_rows(tk)
        for qi in range(nq):
            qT = qT_ref[0, qi, cols, :]
            qs_ref[hh, qi, :, :tq] = jnp.where(row < DA_QK_DIM, qT, zero)
            qs_ref[hh, qi, :, tq:] = jnp.where(row >= DA_QK_DIM, qT, zero)
        chains.append(_Chain(
            load_k=lambda j, r0, rows, cols=cols: jnp.concatenate(
                [k_ref[0, pl.ds(pl.multiple_of(j * tk + r0, LANES), rows), cols],
                 feat_ref[r0:r0 + rows, :]], axis=1),
            load_vT=lambda j, hh=hh: vT_ref[hh, j],
            load_q=lambda qi, diag, hh=hh: jnp.concatenate(
                [qs_ref[hh, qi], gv_ref[diag.astype(jnp.int32)]], axis=0),
            s=s_ref.at[hh], mx=mx_ref.at[hh], p=p_ref.at[hh], a=a_ref.at[hh], m=m_ref.at[hh],
            acc=acc_ref.at[hh]))

    lf = lam_ref[...]
    lam = (jnp.exp(jnp.sum(lf[0:1] * lf[1:2], axis=-1, keepdims=True))
           - jnp.exp(jnp.sum(lf[2:3] * lf[3:4], axis=-1, keepdims=True)) + lam_init)

    def finalize(qi):
        par = qi % 2
        row0 = qi * tq if isinstance(qi, int) else pl.multiple_of(qi * tq, tq)
        for hh in range(HEAD_PAIR):
            acc = acc_ref[hh, par]
            o = acc[:DA_V_DIM] * (1.0 / acc[DA_V_DIM:DA_V_DIM + 1])
            o = (o[:, :tq] - lam * o[:, tq:]).T
            o_ref[0, pl.ds(row0, tq), hh * LANES:(hh + 1) * LANES] = (
                _rms(o, hg_ref[...]) * (1.0 - lam_init)).astype(BF16)

    _flash_flat(chains, nq, finalize)


def _da_attention(lam, hg, qT, k, v, *, tk, q_off, kv_len, lam_init):
    B, nq, _, tq = qT.shape
    T = k.shape[1]
    nkv = T // tk
    _check_tiling(nq=nq, tq=tq, tk=tk, nkv=nkv, q_off=q_off, kv_len=kv_len)
    assert _n_features(tk) <= LANES
    pair = HEAD_PAIR * LANES
    head_kv = lambda b, h: (b, 0, h)
    return pl.pallas_call(
        functools.partial(_da_attn_kernel, nq=nq, tq=tq, tk=tk, nkv=nkv, q_off=q_off, kv_len=kv_len,
                          lam_init=lam_init),
        grid=(B, DA_HEADS // HEAD_PAIR),
        in_specs=[
            pl.BlockSpec(lam.shape, lambda b, h: (0, 0)),
            pl.BlockSpec((1, DA_V_DIM), lambda b, h: (0, 0)),
            pl.BlockSpec((1, nq, pair, tq), lambda b, h: (b, 0, h, 0)),
            pl.BlockSpec((1, T, pair), head_kv),
            pl.BlockSpec((1, T, pair), head_kv),
        ],
        out_specs=pl.BlockSpec((1, nq * tq, pair), head_kv),
        out_shape=jax.ShapeDtypeStruct((B, nq * tq, DA_COLS), BF16),
        scratch_shapes=[
            pltpu.VMEM((HEAD_PAIR, nkv, DA_V_DIM + SUM_PAD, tk), BF16),
            pltpu.VMEM((HEAD_PAIR, nq, LANES, 2 * tq), BF16),
            pltpu.VMEM((tk, LANES), BF16),
            pltpu.VMEM((2, LANES, 2 * tq), BF16),
            pltpu.VMEM((HEAD_PAIR, 2, tk, 2 * tq), F32),
            pltpu.VMEM((HEAD_PAIR, 2, 1, 2 * tq), F32),
            pltpu.VMEM((HEAD_PAIR, 2, tk, 2 * tq), BF16),
            pltpu.VMEM((HEAD_PAIR, 2, 1, 2 * tq), F32),
            pltpu.VMEM((HEAD_PAIR, 1, 2 * tq), F32),
            pltpu.VMEM((HEAD_PAIR, 2, DA_V_DIM + SUM_PAD, 2 * tq), F32),
        ],
        compiler_params=pltpu.CompilerParams(
            dimension_semantics=("arbitrary", "arbitrary"), vmem_limit_bytes=VMEM_LIMIT_BYTES),
        name="da_attention",
    )(lam, hg, qT, k, v)


def _mla_attn_kernel(qT_ref, k_ref, vT_ref, o_ref, gv_ref, s_ref, mx_ref, p_ref, a_ref, m_ref, acc_ref,
                     *, nq, tq, tk, nkv, q_off):
    q_rel = q_off if nkv == 1 else 0
    gv_ref[0] = _query_coeffs(MLA_FEAT, tq, tk, tq, q_rel, diag=False)
    gv_ref[1] = _query_coeffs(MLA_FEAT, tq, tk, tq, q_rel, diag=True)
    chains = []
    for hh in range(HEAD_PAIR):
        rows = slice(hh * MLA_VT_ROWS, (hh + 1) * MLA_VT_ROWS)
        cols = slice(hh * LANES, (hh + 1) * LANES)
        qrows = slice(hh * LANES, hh * LANES + MLA_QK_DIM)
        chains.append(_Chain(
            load_k=lambda j, r0, rows, cols=cols: k_ref[
                0, pl.ds(pl.multiple_of(j * tk + r0, LANES), rows), cols],
            load_vT=lambda j, rows=rows: vT_ref[0, j, rows, :],
            load_q=lambda qi, diag, qrows=qrows: jnp.concatenate(
                [qT_ref[0, qi, qrows, :], gv_ref[diag.astype(jnp.int32)]], axis=0),
            s=s_ref.at[hh], mx=mx_ref.at[hh], p=p_ref.at[hh], a=a_ref.at[hh], m=m_ref.at[hh],
            acc=acc_ref.at[:, rows]))

    def finalize(qi):
        par = qi % 2
        row0 = qi * tq if isinstance(qi, int) else pl.multiple_of(qi * tq, tq)
        o = acc_ref[par]
        halves = [o[hh * MLA_VT_ROWS:hh * MLA_VT_ROWS + MLA_V_DIM]
                  * (1.0 / o[hh * MLA_VT_ROWS + MLA_V_DIM:hh * MLA_VT_ROWS + MLA_V_DIM + 1])
                  for hh in range(HEAD_PAIR)]
        o_ref[0, pl.ds(row0, tq), :] = jnp.concatenate(halves, axis=0).T.astype(BF16)

    _flash_flat(chains, nq, finalize)


def _mla_attention(qmT, km, vT, *, q_off, kv_len):
    B, nq, _, tq = qmT.shape
    nkv, tk = vT.shape[1], vT.shape[3]
    T = km.shape[1]
    _check_tiling(nq=nq, tq=tq, tk=tk, nkv=nkv, q_off=q_off, kv_len=kv_len)
    return pl.pallas_call(
        functools.partial(_mla_attn_kernel, nq=nq, tq=tq, tk=tk, nkv=nkv, q_off=q_off),
        grid=(B, MLA_HEADS // HEAD_PAIR),
        in_specs=[
            pl.BlockSpec((1, nq, HEAD_PAIR * LANES, tq), lambda b, h: (b, 0, h, 0)),
            pl.BlockSpec((1, T, HEAD_PAIR * LANES), lambda b, h: (b, 0, h)),
            pl.BlockSpec((1, nkv, HEAD_PAIR * MLA_VT_ROWS, tk), lambda b, h: (b, 0, h, 0)),
        ],
        out_specs=pl.BlockSpec((1, nq * tq, HEAD_PAIR * MLA_V_DIM), lambda b, h: (b, 0, h)),
        out_shape=jax.ShapeDtypeStruct((B, nq * tq, MLA_WIDTH), BF16),
        scratch_shapes=[
            pltpu.VMEM((2, MLA_FEAT, tq), BF16),
            pltpu.VMEM((HEAD_PAIR, 2, tk, tq), F32),
            pltpu.VMEM((HEAD_PAIR, 2, 1, tq), F32),
            pltpu.VMEM((HEAD_PAIR, 2, tk, tq), BF16),
            pltpu.VMEM((HEAD_PAIR, 2, 1, tq), F32),
            pltpu.VMEM((HEAD_PAIR, 1, tq), F32),
            pltpu.VMEM((2, HEAD_PAIR * MLA_VT_ROWS, tq), F32),
        ],
        compiler_params=pltpu.CompilerParams(
            dimension_semantics=("arbitrary", "arbitrary"), vmem_limit_bytes=VMEM_LIMIT_BYTES),
        name="mla_attention",
    )(qmT, km, vT)


def _epilogue_kernel(x_ref, oa_ref, ob_ref, ng_ref, wzg_ref, gb_ref, wa_ref, wb_ref, wo_ref, fg_ref,
                     y_ref, *, final_norm):
    x = x_ref[...]
    d = x.shape[1]
    h = _rms(x, ng_ref[...]).astype(BF16)
    zg = jnp.dot(h, wzg_ref[...], preferred_element_type=F32)
    ga = (oa_ref[...].astype(F32) * jax.nn.silu(zg[:, :DA_COLS])).astype(BF16)
    gb = (ob_ref[...].astype(F32) * jax.nn.silu(zg[:, DA_COLS:DA_COLS + MLA_WIDTH])).astype(BF16)
    ya = jnp.dot(ga, wa_ref[...], preferred_element_type=F32)
    yb = jnp.dot(gb, wb_ref[...], preferred_element_type=F32)
    g = jax.nn.sigmoid(zg[:, DA_COLS + MLA_WIDTH:] + gb_ref[...])
    m = (g[:, :d] * ya + g[:, d:] * yb).astype(BF16)
    out = x + jnp.dot(m, wo_ref[...], preferred_element_type=F32)
    y_ref[...] = _rms(out, fg_ref[...]) if final_norm else out


def _epilogue(x, oa, ob, norm_g, wzg, gate_b, wa, wb, wo, final_g, *, tm, final_norm):
    n, D = x.shape
    const = lambda i: (0, 0)
    tok = lambda i: (i, 0)
    return pl.pallas_call(
        functools.partial(_epilogue_kernel, final_norm=final_norm),
        grid=(n // tm,),
        in_specs=[
            pl.BlockSpec((tm, D), tok),
            pl.BlockSpec((tm, DA_COLS), tok),
            pl.BlockSpec((tm, MLA_WIDTH), tok),
            pl.BlockSpec((1, D), const),
            pl.BlockSpec(wzg.shape, const),
            pl.BlockSpec((1, 2 * D), const),
            pl.BlockSpec(wa.shape, const),
            pl.BlockSpec(wb.shape, const),
            pl.BlockSpec(wo.shape, const),
            pl.BlockSpec((1, D), const),
        ],
        out_specs=pl.BlockSpec((tm, D), tok),
        out_shape=jax.ShapeDtypeStruct((n, D), F32),
        compiler_params=pltpu.CompilerParams(
            dimension_semantics=("arbitrary",), vmem_limit_bytes=VMEM_LIMIT_BYTES),
        name="epilogue",
    )(x, oa, ob, norm_g, wzg, gate_b, wa, wb, wo, final_g)


def _tiles(S):
    t = min(512, S)
    assert S % t == 0 and t % LANES == 0
    return t


def _prep_weights(w_in, mla_w_uq, mla_w_uk):
    D = w_in.shape[0]
    sizes = (DA_COLS, DA_COLS, DA_COLS, DA_COLS, MLA_Q_LORA, MLA_KV_LORA, MLA_ROPE, MLA_WIDTH, 2 * D)
    assert w_in.shape[1] == sum(sizes)
    w_q, w_k, w_v, w_za, w_cq, w_ckv, w_kr, w_zb, w_g = jnp.split(w_in, np.cumsum(sizes)[:-1].tolist(), axis=1)
    w_kr = jnp.pad(w_kr, ((0, 0), (MLA_NOPE, LANES - MLA_QK_DIM)))
    w_main = jnp.concatenate([w_q, w_k, w_v, w_cq, w_ckv, w_kr], axis=1).astype(BF16)
    w_zg = jnp.concatenate([w_za, w_zb, w_g], axis=1).astype(BF16)
    pad_heads = lambda w, dh: jnp.pad(
        w.reshape(w.shape[0], MLA_HEADS, dh), ((0, 0), (0, 0), (0, LANES - dh))
    ).reshape(w.shape[0], MLA_QK_COLS).astype(BF16)
    return w_main, w_zg, pad_heads(mla_w_uq, MLA_QK_DIM), pad_heads(mla_w_uk, MLA_NOPE)


def _layer(x, past, lam_init, final_norm, norm_g, w_in, gate_b, da_lambda, da_head_norm_g, mla_q_norm_g,
           mla_kv_norm_g, mla_w_uq, mla_w_uk, mla_w_uv, w_branch_a, w_branch_b, w_out, final_norm_g):
    B, S, D = x.shape
    past_len = 0 if past is None else past[0].shape[1]
    Sp = -(-S // LANES) * LANES
    xp = jnp.pad(x, ((0, 0), (0, Sp - S), (0, 0))) if Sp != S else x
    tm = _tiles(Sp)
    pos = past_len + jnp.arange(Sp, dtype=jnp.int32)
    tda = _rope_tables(pos, DA_ROT, _da_lane)
    tml = _rope_tables(pos, MLA_ROPE, _mla_lane)
    w_main, w_zg, wuq, wuk = _prep_weights(w_in, mla_w_uq, mla_w_uk)
    row = lambda v: v.reshape(1, -1)

    qT, k_new, v_new, k_b, v_b, lat_new, kr_new, qmT = _inproj(
        xp, row(norm_g), w_main, row(mla_q_norm_g), row(mla_kv_norm_g), wuq, tda, tml, tm=tm)
    lat_new = lat_new.reshape(B, Sp, MLA_KV_LORA)
    kr_new = kr_new.reshape(B, Sp, LANES)

    kv_len = past_len + S
    if past is None:
        k_all, v_all = k_b.reshape(B, Sp, DA_COLS), v_b.reshape(B, Sp, DA_COLS)
        c_all, kr_all = lat_new, kr_new
        tk = tm
    else:
        pk, pv, pc, pr = past
        tk = -(-kv_len // LANES) * LANES
        cat = lambda old, new: jnp.pad(jnp.concatenate([old, new[:, :S]], axis=1),
                                       ((0, 0), (0, tk - kv_len), (0, 0)))
        k_all = cat(pk.reshape(B, past_len, DA_COLS), k_new.reshape(B, Sp, DA_COLS)).astype(BF16)
        v_all = cat(pv.reshape(B, past_len, DA_COLS), v_new.reshape(B, Sp, DA_COLS)).astype(BF16)
        c_all = cat(pc, lat_new)
        kr_all = cat(jnp.pad(pr, ((0, 0), (0, 0), (MLA_NOPE, LANES - MLA_QK_DIM))), kr_new)

    o_a = _da_attention(da_lambda, row(da_head_norm_g), qT, k_all, v_all,
                        tk=tk, q_off=past_len, kv_len=kv_len, lam_init=lam_init)
    km, vmT = _expand(c_all, kr_all, wuk, mla_w_uv.astype(BF16), tk=tk, kv_len=kv_len)
    o_b = _mla_attention(qmT, km, vmT, q_off=past_len, kv_len=kv_len)

    n = B * Sp
    y = _epilogue(xp.reshape(n, D), o_a.reshape(n, DA_COLS), o_b.reshape(n, MLA_WIDTH), row(norm_g), w_zg,
                  row(gate_b), w_branch_a.astype(BF16), w_branch_b.astype(BF16), w_out.astype(BF16),
                  row(final_norm_g), tm=_tiles(n), final_norm=final_norm)
    y = y.reshape(B, Sp, D)[:, :S]
    new = (k_new[:, :S], v_new[:, :S], lat_new[:, :S], kr_new[:, :S, MLA_NOPE:MLA_QK_DIM])
    return y, new


def kernel(x_prompt, x_sample, cache_da_k, cache_da_v, cache_mla_latent, cache_mla_krope, norm_g, w_in, gate_b, da_lambda, da_head_norm_g, mla_q_norm_g, mla_kv_norm_g, mla_w_uq, mla_w_uk, mla_w_uv, w_branch_a, w_branch_b, w_out, final_norm_g):
    depth = w_in.shape[0]
    hp, hs = x_prompt, x_sample
    rows_p, rows_s = [], []
    for l in range(depth):
        lam_init = 0.8 - 0.6 * math.exp(-0.3 * l)
        last = l == depth - 1
        w = (norm_g[l], w_in[l], gate_b[l], da_lambda[l], da_head_norm_g[l], mla_q_norm_g[l], mla_kv_norm_g[l],
             mla_w_uq[l], mla_w_uk[l], mla_w_uv[l], w_branch_a[l], w_branch_b[l], w_out[l], final_norm_g)
        hp, new_p = _layer(hp, None, lam_init, last, *w)
        past = (cache_da_k[l], cache_da_v[l], cache_mla_latent[l], cache_mla_krope[l])
        hs, new_s = _layer(hs, past, lam_init, last, *w)
        rows_p.append(new_p)
        rows_s.append(new_s)
    stack = lambda rows, i: jnp.stack([r[i] for r in rows], 0)
    return (hp, hs, stack(rows_p, 0), stack(rows_p, 1), stack(rows_p, 2), stack(rows_p, 3),
            stack(rows_s, 0), stack(rows_s, 1), stack(rows_s, 2), stack(rows_s, 3))
```

```python
import functools
import math
from typing import Callable, NamedTuple

import jax
import jax.numpy as jnp
import numpy as np
from jax import lax
from jax.experimental import pallas as pl
from jax.experimental.pallas import tpu as pltpu

CHUNK = 64
ROPE_THETA = 500000.0
RMS_EPS = 1e-6
DA_HEADS = 4
DA_QK_DIM = 64
DA_V_DIM = 2 * DA_QK_DIM
DA_ROT = DA_QK_DIM // 4
DA_COLS = DA_HEADS * DA_V_DIM
MLA_HEADS = 8
MLA_Q_LORA = 384
MLA_KV_LORA = 256
MLA_NOPE = 64
MLA_ROPE = 32
MLA_V_DIM = 64
MLA_WIDTH = MLA_HEADS * MLA_V_DIM
MLA_QK_DIM = MLA_NOPE + MLA_ROPE

LANES = 128
SUM_ROWS = 8
BF16_ROWS = 16
SUM_PAD = BF16_ROWS
MLA_VT_ROWS = 64 + SUM_PAD
MLA_QK_COLS = MLA_HEADS * LANES
MLA_FEAT = LANES - MLA_QK_DIM
LOG2E = math.log2(math.e)
NEG_BIG = -1e30
VMEM_LIMIT_BYTES = 56 * 1024 * 1024
HEAD_PAIR = 2
MLA_GROUP = 2
LOOP_STAGES = 2

F32 = jnp.float32
BF16 = jnp.bfloat16


def _rms(x, g):
    return x * lax.rsqrt(jnp.mean(x * x, axis=-1, keepdims=True) + RMS_EPS) * g


def _rope128(x, tab_ref, shift):
    return (x * tab_ref[0] + pltpu.roll(x, LANES - shift, 1) * tab_ref[1]
            + pltpu.roll(x, shift, 1) * tab_ref[2])


def _rope_tables(pos, rot, lane_of):
    half = rot // 2
    inv = jnp.float32(ROPE_THETA) ** (-jnp.arange(half, dtype=F32) * 2.0 / rot)
    ang = pos.astype(F32)[:, None] * inv
    cos, sin = jnp.cos(ang), jnp.sin(ang)
    idx = np.array([lane_of(j) for j in range(LANES)])
    lo = (idx >= 0) & (idx < half)
    hi = idx >= half
    src = np.where(idx >= 0, idx % half, 0)
    c = jnp.where(jnp.asarray(lo | hi), cos[:, src], 1.0)
    sa = jnp.where(jnp.asarray(lo), -sin[:, src], 0.0)
    sb = jnp.where(jnp.asarray(hi), sin[:, src], 0.0)
    return jnp.stack([c, sa, sb]).astype(F32)


def _da_lane(j):
    c = j % DA_QK_DIM
    return c if c < DA_ROT else -1


def _mla_lane(j):
    c = j - MLA_NOPE
    return c if 0 <= c < MLA_ROPE else -1


def _inproj_kernel(x_ref, ng_ref, wm_ref, qg_ref, kvg_ref, wuq_ref, tda_ref, tml_ref, *refs,
                   q_scale, qm_scale, expand):
    if expand:
        wuk_ref, wuv_ref = refs[:2]
        refs = refs[2:]
    qT_ref, k4_ref, v4_ref, kb_ref, vb_ref, lat_ref, kr_ref, kr32_ref, qmT_ref = refs[:9]
    o_k, o_v = DA_COLS, 2 * DA_COLS
    o_cq = 3 * DA_COLS
    o_ckv = o_cq + MLA_Q_LORA
    o_kr = o_ckv + MLA_KV_LORA
    h = _rms(x_ref[...], ng_ref[...]).astype(BF16)
    proj = jnp.dot(h, wm_ref[...], preferred_element_type=F32)
    for i in range(DA_HEADS):
        lo, hi = i * LANES, (i + 1) * LANES
        q = _rope128(proj[:, lo:hi], tda_ref, DA_ROT // 2) * q_scale
        qT_ref[0, 0, lo:hi, :] = q.T.astype(BF16)
        k = _rope128(proj[:, o_k + lo:o_k + hi], tda_ref, DA_ROT // 2)
        v = proj[:, o_v + lo:o_v + hi]
        k4_ref[0, :, i, :] = k
        v4_ref[0, :, i, :] = v
        kb_ref[:, lo:hi] = k.astype(BF16)
        vb_ref[:, lo:hi] = v.astype(BF16)
    lat = _rms(proj[:, o_ckv:o_ckv + MLA_KV_LORA], kvg_ref[...])
    lat_ref[...] = lat
    kr = _rope128(proj[:, o_kr:o_kr + LANES], tml_ref, MLA_ROPE // 2)
    kr_ref[...] = kr
    kr32_ref[...] = kr[:, MLA_NOPE:MLA_QK_DIM]
    if expand:
        _expand_tile(lat, kr, wuk_ref, wuv_ref, refs[9], refs[10], valid_rows=x_ref.shape[0])
    qn = _rms(proj[:, o_cq:o_cq + MLA_Q_LORA], qg_ref[...]).astype(BF16)
    qm = jnp.dot(qn, wuq_ref[...], preferred_element_type=F32)
    coef = (tml_ref[1] + tml_ref[2]) * qm_scale
    cos = tml_ref[0] * qm_scale
    for i in range(MLA_HEADS):
        lo, hi = i * LANES, (i + 1) * LANES
        q = qm[:, lo:hi] * cos + qm[:, MLA_QK_COLS + lo:MLA_QK_COLS + hi] * coef
        qmT_ref[0, 0, lo:hi, :] = q.T.astype(BF16)


def _inproj(x, norm_g, w_main, qg, kvg, wuq, tda, tml, wuk, wuv, *, tm, expand):
    B, S, D = x.shape
    n = B * S
    nst = S // tm
    wcols = w_main.shape[1]
    const = lambda i: (0, 0)
    tok = lambda i: (i, 0)
    feat_t = lambda i: (i // nst, i % nst, 0, 0)
    tab = lambda i: (0, i % nst, 0)
    in_specs = [
        pl.BlockSpec((tm, D), tok),
        pl.BlockSpec((1, D), const),
        pl.BlockSpec((D, wcols), const),
        pl.BlockSpec((1, MLA_Q_LORA), const),
        pl.BlockSpec((1, MLA_KV_LORA), const),
        pl.BlockSpec((MLA_Q_LORA, 2 * MLA_QK_COLS), const),
        pl.BlockSpec((3, tm, LANES), tab),
        pl.BlockSpec((3, tm, LANES), tab),
    ]
    out_specs = [
        pl.BlockSpec((1, 1, DA_COLS, tm), feat_t),
        pl.BlockSpec((1, tm, DA_HEADS, DA_V_DIM), feat_t),
        pl.BlockSpec((1, tm, DA_HEADS, DA_V_DIM), feat_t),
        pl.BlockSpec((tm, DA_COLS), tok),
        pl.BlockSpec((tm, DA_COLS), tok),
        pl.BlockSpec((tm, MLA_KV_LORA), tok),
        pl.BlockSpec((tm, LANES), tok),
        pl.BlockSpec((tm, MLA_ROPE), tok),
        pl.BlockSpec((1, 1, MLA_QK_COLS, tm), feat_t),
    ]
    out_shape = [
        jax.ShapeDtypeStruct((B, nst, DA_COLS, tm), BF16),
        jax.ShapeDtypeStruct((B, S, DA_HEADS, DA_V_DIM), F32),
        jax.ShapeDtypeStruct((B, S, DA_HEADS, DA_V_DIM), F32),
        jax.ShapeDtypeStruct((n, DA_COLS), BF16),
        jax.ShapeDtypeStruct((n, DA_COLS), BF16),
        jax.ShapeDtypeStruct((n, MLA_KV_LORA), F32),
        jax.ShapeDtypeStruct((n, LANES), F32),
        jax.ShapeDtypeStruct((n, MLA_ROPE), F32),
        jax.ShapeDtypeStruct((B, nst, MLA_QK_COLS, tm), BF16),
    ]
    args = [x.reshape(n, D), norm_g, w_main, qg, kvg, wuq, tda, tml]
    if expand:
        assert _n_features(tm) <= MLA_FEAT
        in_specs += [pl.BlockSpec(wuk.shape, const), pl.BlockSpec(wuv.shape, const)]
        out_specs += [pl.BlockSpec((tm, MLA_QK_COLS), tok),
                      pl.BlockSpec((1, 1, MLA_HEADS * MLA_VT_ROWS, tm), feat_t)]
        out_shape += [jax.ShapeDtypeStruct((n, MLA_QK_COLS), BF16),
                      jax.ShapeDtypeStruct((B, nst, MLA_HEADS * MLA_VT_ROWS, tm), BF16)]
        args += [wuk, wuv]
    return pl.pallas_call(
        functools.partial(_inproj_kernel, q_scale=DA_QK_DIM ** -0.5 * LOG2E,
                          qm_scale=MLA_QK_DIM ** -0.5 * LOG2E, expand=expand),
        grid=(n // tm,),
        in_specs=in_specs,
        out_specs=out_specs,
        out_shape=out_shape,
        compiler_params=pltpu.CompilerParams(
            dimension_semantics=("arbitrary",), vmem_limit_bytes=VMEM_LIMIT_BYTES),
        name="inproj",
    )(*args)


def _n_features(tk):
    return tk // CHUNK + 1


def _key_features(tk, valid_rows, lane0, dtype):
    nf = _n_features(tk)
    row = lax.broadcasted_iota(jnp.int32, (tk, LANES), 0)
    f = lax.broadcasted_iota(jnp.int32, (tk, LANES), 1) - lane0
    chunk_hit = (f >= 0) & (f < nf - 1) & (lax.shift_right_logical(row, 6) == f)
    pad_hit = (f == nf - 1) & (row >= valid_rows)
    return jnp.where(chunk_hit | pad_hit, 1.0, 0.0).astype(dtype)


def _query_coeffs(nrows, ncols, tk, tq, q_rel, diag):
    nf = _n_features(tk)
    r = lax.broadcasted_iota(jnp.int32, (nrows, ncols), 0)
    col = lax.broadcasted_iota(jnp.int32, (nrows, ncols), 1)
    q_chunk = lax.shift_right_logical(q_rel + jnp.where(col >= tq, col - tq, col), 6)
    hidden = (r == nf - 1)
    if diag:
        hidden = hidden | ((r < nf - 1) & (r > q_chunk))
    return jnp.where(hidden, NEG_BIG, 0.0).astype(BF16)


def _expand_tile(lat, kr, wuk_ref, wuv_ref, km_ref, vT_ref, *, valid_rows):
    tk = lat.shape[0]
    c = lat.astype(BF16)
    kn = jnp.dot(c, wuk_ref[...], preferred_element_type=F32)
    tail = kr + _key_features(tk, valid_rows, MLA_QK_DIM, F32)
    for i in range(MLA_HEADS):
        lo, hi = i * LANES, (i + 1) * LANES
        km_ref[:, lo:hi] = (kn[:, lo:hi] + tail).astype(BF16)
    vm = jnp.dot(c, wuv_ref[...], preferred_element_type=F32)
    ones = _ones_rows(tk)
    for i in range(MLA_WIDTH // LANES):
        vt = vm[:, i * LANES:(i + 1) * LANES].T.astype(BF16)
        for hh in range(LANES // MLA_V_DIM):
            r0 = (i * (LANES // MLA_V_DIM) + hh) * MLA_VT_ROWS
            vT_ref[0, 0, r0:r0 + MLA_V_DIM, :] = vt[hh * MLA_V_DIM:(hh + 1) * MLA_V_DIM]
            vT_ref[0, 0, r0 + MLA_V_DIM:r0 + MLA_VT_ROWS, :] = ones


def _expand_kernel(c_ref, kr_ref, wuk_ref, wuv_ref, km_ref, vT_ref, *, valid_rows):
    _expand_tile(c_ref[0], kr_ref[0], wuk_ref, wuv_ref, km_ref.at[0], vT_ref, valid_rows=valid_rows)


def _expand(c, kr, wuk, wuv, *, tk, kv_len):
    B, T, _ = c.shape
    nkv = T // tk
    assert _n_features(tk) <= MLA_FEAT
    const = lambda b, j: (0, 0)
    tile = lambda b, j: (b, j, 0)
    return pl.pallas_call(
        functools.partial(_expand_kernel, valid_rows=kv_len - (nkv - 1) * tk),
        grid=(B, nkv),
        in_specs=[
            pl.BlockSpec((1, tk, MLA_KV_LORA), tile),
            pl.BlockSpec((1, tk, LANES), tile),
            pl.BlockSpec((MLA_KV_LORA, MLA_QK_COLS), const),
            pl.BlockSpec((MLA_KV_LORA, MLA_WIDTH), const),
        ],
        out_specs=[
            pl.BlockSpec((1, tk, MLA_QK_COLS), tile),
            pl.BlockSpec((1, 1, MLA_HEADS * MLA_VT_ROWS, tk), lambda b, j: (b, j, 0, 0)),
        ],
        out_shape=[
            jax.ShapeDtypeStruct((B, T, MLA_QK_COLS), BF16),
            jax.ShapeDtypeStruct((B, nkv, MLA_HEADS * MLA_VT_ROWS, tk), BF16),
        ],
        compiler_params=pltpu.CompilerParams(
            dimension_semantics=("arbitrary", "arbitrary"), vmem_limit_bytes=VMEM_LIMIT_BYTES),
        name="kv_expand",
    )(c, kr, wuk, wuv)


def _check_tiling(*, nq, tq, tk, nkv, q_off, kv_len):
    single = nq == 1 and nkv == 1
    causal = q_off == 0 and tq == tk and nq == nkv and kv_len == nkv * tk and (nq * (nq + 1) // 2) % 2 == 0
    assert single or causal, (nq, tq, tk, nkv, q_off, kv_len)


class _Chain(NamedTuple):
    load_k: Callable
    load_vT: Callable
    load_q: Callable
    s: object
    mx: object
    p: object
    a: object
    m: object
    acc: object


def _flash_flat(chains, nq, finalize):
    for c in chains:
        tk = c.s.shape[1]
        c.m[...] = jnp.full(c.m.shape, NEG_BIG, F32)
        c.acc[...] = jnp.zeros(c.acc.shape, F32)

    def advance(u):
        qi, j = u
        last = j == qi
        return jnp.where(last, qi + 1, qi), jnp.where(last, 0, j + 1)

    def scores(u, slot):
        qi, j = u
        for c in chains:
            s = jnp.dot(c.load_k(j, 0, tk), c.load_q(qi, j == qi), preferred_element_type=F32)
            c.s[slot] = s
            c.mx[slot] = jnp.max(s, axis=0, keepdims=True)

    def softmax(u, slot):
        qi, j = u
        for c in chains:
            m_prev = jnp.where(j == 0, NEG_BIG, c.m[...])
            m_new = jnp.maximum(m_prev, c.mx[slot])
            c.m[...] = m_new
            c.a[slot] = jnp.exp2(m_prev - m_new)
            for r in range(0, tk, BF16_ROWS):
                x = (c.s[slot, r:r + BF16_ROWS, :] - m_new).astype(BF16)
                c.p[slot, r:r + BF16_ROWS, :] = jnp.exp2(x)

    def values(u, slot):
        qi, j = u
        par = lax.rem(qi, 2)
        for c in chains:
            pv = jnp.dot(c.load_vT(j), c.p[slot], preferred_element_type=F32)
            c.acc[par] = c.a[slot] * c.acc[par] + pv

    def stages(ua, count):
        units = [ua]
        for _ in range(count + 1):
            units.append(advance(units[-1]))
        for i in range(count):
            cur, nxt = (i + 1) % 2, i % 2
            scores(units[i + 2], nxt)
            softmax(units[i + 1], cur)
            values(units[i], nxt)
        for i in range(0, count, 2):
            done_a = units[i][1] == units[i][0]
            done_b = units[i + 1][1] == units[i + 1][0]

            @pl.when(jnp.logical_or(done_a, done_b))
            def _():
                finalize(jnp.where(done_a, units[i][0], units[i + 1][0]))

        return units[count]

    zero = jnp.int32(0)
    u0 = (zero, zero)
    n_units = nq * (nq + 1) // 2
    if n_units == 1:
        scores(u0, 0)
        softmax(u0, 0)
        values(u0, 0)
        finalize(0)
        return

    scores(u0, 0)
    scores(advance(u0), 1)
    softmax(u0, 0)

    ua = lax.fori_loop(0, (n_units - 2) // LOOP_STAGES, lambda t, ua: stages(ua, LOOP_STAGES), u0)
    rest = (n_units - 2) % LOOP_STAGES
    if rest:
        ua = stages(ua, rest)
    ub = advance(ua)
    softmax(ub, 1)
    values(ua, 0)
    values(ub, 1)
    finalize(nq - 1)


def _ones_rows(tk):
    row = lax.broadcasted_iota(jnp.int32, (SUM_PAD, tk), 0)
    return jnp.where(row == 0, 1.0, 0.0).astype(BF16)


def _da_attn_kernel(lam_ref, hg_ref, qT_ref, k_ref, v_ref, o_ref,
                    vT_ref, qs_ref, feat_ref, gv_ref, s_ref, mx_ref, p_ref, a_ref, m_ref, acc_ref,
                    *, nq, tq, tk, nkv, q_off, kv_len, lam_init):
    valid_rows = kv_len - (nkv - 1) * tk
    feat_ref[...] = _key_features(tk, valid_rows, 0, BF16)
    q_rel = q_off if nkv == 1 else 0
    gv_ref[0] = _query_coeffs(LANES, 2 * tq, tk, tq, q_rel, diag=False)
    gv_ref[1] = _query_coeffs(LANES, 2 * tq, tk, tq, q_rel, diag=True)

    row = lax.broadcasted_iota(jnp.int32, (LANES, tq), 0)
    zero = jnp.zeros((LANES, tq), BF16)
    chains = []
    for hh in range(HEAD_PAIR):
        cols = slice(hh * LANES, (hh + 1) * LANES)
        for c in range(nkv):
            vT_ref[hh, c, :DA_V_DIM] = v_ref[0, c * tk:(c + 1) * tk, cols].astype(F32).T.astype(BF16)
            vT_ref[hh, c, DA_V_DIM:] = _ones_rows(tk)
        for qi in range(nq):
            qT = qT_ref[0, qi, cols, :]
            qs_ref[hh, qi, :, :tq] = jnp.where(row < DA_QK_DIM, qT, zero)
            qs_ref[hh, qi, :, tq:] = jnp.where(row >= DA_QK_DIM, qT, zero)
        chains.append(_Chain(
            load_k=lambda j, r0, rows, cols=cols: jnp.concatenate(
                [k_ref[0, pl.ds(pl.multiple_of(j * tk + r0, LANES), rows), cols],
                 feat_ref[r0:r0 + rows, :]], axis=1),
            load_vT=lambda j, hh=hh: vT_ref[hh, j],
            load_q=lambda qi, diag, hh=hh: jnp.concatenate(
                [qs_ref[hh, qi], gv_ref[diag.astype(jnp.int32)]], axis=0),
            s=s_ref.at[hh], mx=mx_ref.at[hh], p=p_ref.at[hh], a=a_ref.at[hh], m=m_ref.at[hh],
            acc=acc_ref.at[hh]))

    lf = lam_ref[...]
    lam = (jnp.exp(jnp.sum(lf[0:1] * lf[1:2], axis=-1, keepdims=True))
           - jnp.exp(jnp.sum(lf[2:3] * lf[3:4], axis=-1, keepdims=True)) + lam_init)

    def finalize(qi):
        par = qi % 2
        row0 = qi * tq if isinstance(qi, int) else pl.multiple_of(qi * tq, tq)
        for hh in range(HEAD_PAIR):
            acc = acc_ref[hh, par]
            o = acc[:DA_V_DIM] * (1.0 / acc[DA_V_DIM:DA_V_DIM + 1])
            o = (o[:, :tq] - lam * o[:, tq:]).T
            o_ref[0, pl.ds(row0, tq), hh * LANES:(hh + 1) * LANES] = (
                _rms(o, hg_ref[...]) * (1.0 - lam_init)).astype(BF16)

    _flash_flat(chains, nq, finalize)


def _da_attention(lam, hg, qT, k, v, *, tk, q_off, kv_len, lam_init):
    B, nq, _, tq = qT.shape
    T = k.shape[1]
    nkv = T // tk
    _check_tiling(nq=nq, tq=tq, tk=tk, nkv=nkv, q_off=q_off, kv_len=kv_len)
    assert _n_features(tk) <= LANES
    pair = HEAD_PAIR * LANES
    head_kv = lambda b, h: (b, 0, h)
    return pl.pallas_call(
        functools.partial(_da_attn_kernel, nq=nq, tq=tq, tk=tk, nkv=nkv, q_off=q_off, kv_len=kv_len,
                          lam_init=lam_init),
        grid=(B, DA_HEADS // HEAD_PAIR),
        in_specs=[
            pl.BlockSpec(lam.shape, lambda b, h: (0, 0)),
            pl.BlockSpec((1, DA_V_DIM), lambda b, h: (0, 0)),
            pl.BlockSpec((1, nq, pair, tq), lambda b, h: (b, 0, h, 0)),
            pl.BlockSpec((1, T, pair), head_kv),
            pl.BlockSpec((1, T, pair), head_kv),
        ],
        out_specs=pl.BlockSpec((1, nq * tq, pair), head_kv),
        out_shape=jax.ShapeDtypeStruct((B, nq * tq, DA_COLS), BF16),
        scratch_shapes=[
            pltpu.VMEM((HEAD_PAIR, nkv, DA_V_DIM + SUM_PAD, tk), BF16),
            pltpu.VMEM((HEAD_PAIR, nq, LANES, 2 * tq), BF16),
            pltpu.VMEM((tk, LANES), BF16),
            pltpu.VMEM((2, LANES, 2 * tq), BF16),
            pltpu.VMEM((HEAD_PAIR, 2, tk, 2 * tq), F32),
            pltpu.VMEM((HEAD_PAIR, 2, 1, 2 * tq), F32),
            pltpu.VMEM((HEAD_PAIR, 2, tk, 2 * tq), BF16),
            pltpu.VMEM((HEAD_PAIR, 2, 1, 2 * tq), F32),
            pltpu.VMEM((HEAD_PAIR, 1, 2 * tq), F32),
            pltpu.VMEM((HEAD_PAIR, 2, DA_V_DIM + SUM_PAD, 2 * tq), F32),
        ],
        compiler_params=pltpu.CompilerParams(
            dimension_semantics=("arbitrary", "arbitrary"), vmem_limit_bytes=VMEM_LIMIT_BYTES),
        name="da_attention",
    )(lam, hg, qT, k, v)


def _mla_attn_kernel(qT_ref, k_ref, vT_ref, o_ref, gv_ref, s_ref, mx_ref, p_ref, a_ref, m_ref, acc_ref,
                     *, nq, tq, tk, nkv, q_off):
    q_rel = q_off if nkv == 1 else 0
    gv_ref[0] = _query_coeffs(MLA_FEAT, tq, tk, tq, q_rel, diag=False)
    gv_ref[1] = _query_coeffs(MLA_FEAT, tq, tk, tq, q_rel, diag=True)
    chains = []
    for hh in range(MLA_GROUP):
        rows = slice(hh * MLA_VT_ROWS, (hh + 1) * MLA_VT_ROWS)
        cols = slice(hh * LANES, (hh + 1) * LANES)
        qrows = slice(hh * LANES, hh * LANES + MLA_QK_DIM)
        chains.append(_Chain(
            load_k=lambda j, r0, rows, cols=cols: k_ref[
                0, pl.ds(pl.multiple_of(j * tk + r0, LANES), rows), cols],
            load_vT=lambda j, rows=rows: vT_ref[0, j, rows, :],
            load_q=lambda qi, diag, qrows=qrows: jnp.concatenate(
                [qT_ref[0, qi, qrows, :], gv_ref[diag.astype(jnp.int32)]], axis=0),
            s=s_ref.at[hh], mx=mx_ref.at[hh], p=p_ref.at[hh], a=a_ref.at[hh], m=m_ref.at[hh],
            acc=acc_ref.at[:, rows]))

    def finalize(qi):
        par = qi % 2
        row0 = qi * tq if isinstance(qi, int) else pl.multiple_of(qi * tq, tq)
        o = acc_ref[par]
        halves = [o[hh * MLA_VT_ROWS:hh * MLA_VT_ROWS + MLA_V_DIM]
                  * (1.0 / o[hh * MLA_VT_ROWS + MLA_V_DIM:hh * MLA_VT_ROWS + MLA_V_DIM + 1])
                  for hh in range(MLA_GROUP)]
        for i in range(0, MLA_GROUP, 2):
            o_ref[0, pl.ds(row0, tq), i * MLA_V_DIM:(i + 2) * MLA_V_DIM] = (
                jnp.concatenate(halves[i:i + 2], axis=0).T.astype(BF16))

    _flash_flat(chains, nq, finalize)


def _mla_attention(qmT, km, vT, *, q_off, kv_len):
    B, nq, _, tq = qmT.shape
    nkv, tk = vT.shape[1], vT.shape[3]
    T = km.shape[1]
    _check_tiling(nq=nq, tq=tq, tk=tk, nkv=nkv, q_off=q_off, kv_len=kv_len)
    return pl.pallas_call(
        functools.partial(_mla_attn_kernel, nq=nq, tq=tq, tk=tk, nkv=nkv, q_off=q_off),
        grid=(B, MLA_HEADS // MLA_GROUP),
        in_specs=[
            pl.BlockSpec((1, nq, MLA_GROUP * LANES, tq), lambda b, h: (b, 0, h, 0)),
            pl.BlockSpec((1, T, MLA_GROUP * LANES), lambda b, h: (b, 0, h)),
            pl.BlockSpec((1, nkv, MLA_GROUP * MLA_VT_ROWS, tk), lambda b, h: (b, 0, h, 0)),
        ],
        out_specs=pl.BlockSpec((1, nq * tq, MLA_GROUP * MLA_V_DIM), lambda b, h: (b, 0, h)),
        out_shape=jax.ShapeDtypeStruct((B, nq * tq, MLA_WIDTH), BF16),
        scratch_shapes=[
            pltpu.VMEM((2, MLA_FEAT, tq), BF16),
            pltpu.VMEM((MLA_GROUP, 2, tk, tq), F32),
            pltpu.VMEM((MLA_GROUP, 2, 1, tq), F32),
            pltpu.VMEM((MLA_GROUP, 2, tk, tq), BF16),
            pltpu.VMEM((MLA_GROUP, 2, 1, tq), F32),
            pltpu.VMEM((MLA_GROUP, 1, tq), F32),
            pltpu.VMEM((2, MLA_GROUP * MLA_VT_ROWS, tq), F32),
        ],
        compiler_params=pltpu.CompilerParams(
            dimension_semantics=("arbitrary", "arbitrary"), vmem_limit_bytes=VMEM_LIMIT_BYTES),
        name="mla_attention",
    )(qmT, km, vT)


def _epilogue_kernel(x_ref, oa_ref, ob_ref, ng_ref, wzg_ref, gb_ref, wa_ref, wb_ref, wo_ref, fg_ref,
                     y_ref, *, final_norm):
    x = x_ref[...]
    d = x.shape[1]
    h = _rms(x, ng_ref[...]).astype(BF16)
    zg = jnp.dot(h, wzg_ref[...], preferred_element_type=F32)
    ga = (oa_ref[...].astype(F32) * jax.nn.silu(zg[:, :DA_COLS])).astype(BF16)
    gb = (ob_ref[...].astype(F32) * jax.nn.silu(zg[:, DA_COLS:DA_COLS + MLA_WIDTH])).astype(BF16)
    ya = jnp.dot(ga, wa_ref[...], preferred_element_type=F32)
    yb = jnp.dot(gb, wb_ref[...], preferred_element_type=F32)
    g = jax.nn.sigmoid(zg[:, DA_COLS + MLA_WIDTH:] + gb_ref[...])
    m = (g[:, :d] * ya + g[:, d:] * yb).astype(BF16)
    out = x + jnp.dot(m, wo_ref[...], preferred_element_type=F32)
    y_ref[...] = _rms(out, fg_ref[...]) if final_norm else out


def _epilogue(x, oa, ob, norm_g, wzg, gate_b, wa, wb, wo, final_g, *, tm, final_norm):
    n, D = x.shape
    const = lambda i: (0, 0)
    tok = lambda i: (i, 0)
    return pl.pallas_call(
        functools.partial(_epilogue_kernel, final_norm=final_norm),
        grid=(n // tm,),
        in_specs=[
            pl.BlockSpec((tm, D), tok),
            pl.BlockSpec((tm, DA_COLS), tok),
            pl.BlockSpec((tm, MLA_WIDTH), tok),
            pl.BlockSpec((1, D), const),
            pl.BlockSpec(wzg.shape, const),
            pl.BlockSpec((1, 2 * D), const),
            pl.BlockSpec(wa.shape, const),
            pl.BlockSpec(wb.shape, const),
            pl.BlockSpec(wo.shape, const),
            pl.BlockSpec((1, D), const),
        ],
        out_specs=pl.BlockSpec((tm, D), tok),
        out_shape=jax.ShapeDtypeStruct((n, D), F32),
        compiler_params=pltpu.CompilerParams(
            dimension_semantics=("arbitrary",), vmem_limit_bytes=VMEM_LIMIT_BYTES),
        name="epilogue",
    )(x, oa, ob, norm_g, wzg, gate_b, wa, wb, wo, final_g)


def _tiles(S):
    t = min(512, S)
    assert S % t == 0 and t % LANES == 0
    return t


def _prep_weights(w_in, mla_w_uq, mla_w_uk):
    D = w_in.shape[0]
    sizes = (DA_COLS, DA_COLS, DA_COLS, DA_COLS, MLA_Q_LORA, MLA_KV_LORA, MLA_ROPE, MLA_WIDTH, 2 * D)
    assert w_in.shape[1] == sum(sizes)
    w_q, w_k, w_v, w_za, w_cq, w_ckv, w_kr, w_zb, w_g = jnp.split(w_in, np.cumsum(sizes)[:-1].tolist(), axis=1)
    w_kr = jnp.pad(w_kr, ((0, 0), (MLA_NOPE, LANES - MLA_QK_DIM)))
    w_main = jnp.concatenate([w_q, w_k, w_v, w_cq, w_ckv, w_kr], axis=1).astype(BF16)
    w_zg = jnp.concatenate([w_za, w_zb, w_g], axis=1).astype(BF16)
    pad_heads = lambda w, dh: jnp.pad(
        w.reshape(w.shape[0], MLA_HEADS, dh), ((0, 0), (0, 0), (0, LANES - dh))
    ).reshape(w.shape[0], MLA_QK_COLS).astype(BF16)
    wuq = pad_heads(mla_w_uq, MLA_QK_DIM)
    lane = np.arange(LANES)
    rot = np.array([_mla_lane(j) for j in range(LANES)])
    half = MLA_ROPE // 2
    partner = np.where(rot < 0, lane, np.where(rot < half, lane + half, lane - half))
    cols = (np.arange(MLA_HEADS)[:, None] * LANES + partner[None, :]).reshape(-1)
    wuq_rot = jnp.where(jnp.asarray(np.tile(rot >= 0, MLA_HEADS)), wuq[:, cols], 0).astype(BF16)
    return w_main, w_zg, jnp.concatenate([wuq, wuq_rot], axis=1), pad_heads(mla_w_uk, MLA_NOPE)


class _LayerWeights(NamedTuple):
    norm_g: jax.Array
    w_main: jax.Array
    w_zg: jax.Array
    gate_b: jax.Array
    da_lambda: jax.Array
    da_head_norm_g: jax.Array
    mla_q_norm_g: jax.Array
    mla_kv_norm_g: jax.Array
    wuq: jax.Array
    wuk: jax.Array
    wuv: jax.Array
    wa: jax.Array
    wb: jax.Array
    wo: jax.Array
    final_norm_g: jax.Array


def _layer(x, past, lam_init, final_norm, w):
    B, S, D = x.shape
    past_len = 0 if past is None else past[0].shape[1]
    Sp = -(-S // LANES) * LANES
    xp = jnp.pad(x, ((0, 0), (0, Sp - S), (0, 0))) if Sp != S else x
    tm = _tiles(Sp)
    pos = past_len + jnp.arange(Sp, dtype=jnp.int32)
    tda = _rope_tables(pos, DA_ROT, _da_lane)
    tml = _rope_tables(pos, MLA_ROPE, _mla_lane)

    outs = _inproj(xp, w.norm_g, w.w_main, w.mla_q_norm_g, w.mla_kv_norm_g, w.wuq, tda, tml, w.wuk, w.wuv,
                   tm=tm, expand=past is None)
    qT, k_new, v_new, k_b, v_b, lat_new, kr_new, kr32, qmT = outs[:9]
    lat_new = lat_new.reshape(B, Sp, MLA_KV_LORA)

    kv_len = past_len + S
    if past is None:
        k_all, v_all = k_b.reshape(B, Sp, DA_COLS), v_b.reshape(B, Sp, DA_COLS)
        km, vmT = outs[9].reshape(B, Sp, MLA_QK_COLS), outs[10]
        tk = tm
    else:
        pk, pv, pc, pr = past
        tk = -(-kv_len // LANES) * LANES
        cat = lambda old, new: jnp.pad(jnp.concatenate([old, new[:, :S]], axis=1),
                                       ((0, 0), (0, tk - kv_len), (0, 0)))
        k_all = cat(pk.reshape(B, past_len, DA_COLS), k_new.reshape(B, Sp, DA_COLS)).astype(BF16)
        v_all = cat(pv.reshape(B, past_len, DA_COLS), v_new.reshape(B, Sp, DA_COLS)).astype(BF16)
        c_all = cat(pc, lat_new)
        kr_all = cat(jnp.pad(pr, ((0, 0), (0, 0), (MLA_NOPE, LANES - MLA_QK_DIM))), kr_new.reshape(B, Sp, LANES))
        km, vmT = _expand(c_all, kr_all, w.wuk, w.wuv, tk=tk, kv_len=kv_len)

    o_a = _da_attention(w.da_lambda, w.da_head_norm_g, qT, k_all, v_all,
                        tk=tk, q_off=past_len, kv_len=kv_len, lam_init=lam_init)
    o_b = _mla_attention(qmT, km, vmT, q_off=past_len, kv_len=kv_len)

    n = B * Sp
    y = _epilogue(xp.reshape(n, D), o_a.reshape(n, DA_COLS), o_b.reshape(n, MLA_WIDTH), w.norm_g, w.w_zg,
                  w.gate_b, w.wa, w.wb, w.wo, w.final_norm_g, tm=_tiles(n), final_norm=final_norm)
    y = y.reshape(B, Sp, D)[:, :S]
    new = (k_new[:, :S], v_new[:, :S], lat_new[:, :S], kr32.reshape(B, Sp, MLA_ROPE)[:, :S])
    return y, new


def kernel(x_prompt, x_sample, cache_da_k, cache_da_v, cache_mla_latent, cache_mla_krope, norm_g, w_in, gate_b, da_lambda, da_head_norm_g, mla_q_norm_g, mla_kv_norm_g, mla_w_uq, mla_w_uk, mla_w_uv, w_branch_a, w_branch_b, w_out, final_norm_g):
    depth = w_in.shape[0]
    at = lambda a, l: a.reshape(a.shape[1:]) if depth == 1 else a[l]
    row = lambda v: v.reshape(1, -1)
    hp, hs = x_prompt, x_sample
    rows_p, rows_s = [], []
    for l in range(depth):
        lam_init = 0.8 - 0.6 * math.exp(-0.3 * l)
        last = l == depth - 1
        w_main, w_zg, wuq, wuk = _prep_weights(at(w_in, l), at(mla_w_uq, l), at(mla_w_uk, l))
        w = _LayerWeights(
            norm_g=row(at(norm_g, l)), w_main=w_main, w_zg=w_zg, gate_b=row(at(gate_b, l)),
            da_lambda=at(da_lambda, l), da_head_norm_g=row(at(da_head_norm_g, l)),
            mla_q_norm_g=row(at(mla_q_norm_g, l)), mla_kv_norm_g=row(at(mla_kv_norm_g, l)),
            wuq=wuq, wuk=wuk, wuv=at(mla_w_uv, l).astype(BF16), wa=at(w_branch_a, l).astype(BF16),
            wb=at(w_branch_b, l).astype(BF16), wo=at(w_out, l).astype(BF16), final_norm_g=row(final_norm_g))
        hp, new_p = _layer(hp, None, lam_init, last, w)
        past = (at(cache_da_k, l), at(cache_da_v, l), at(cache_mla_latent, l), at(cache_mla_krope, l))
        hs, new_s = _layer(hs, past, lam_init, last, w)
        rows_p.append(new_p)
        rows_s.append(new_s)
    stack = lambda rows, i: jnp.stack([r[i] for r in rows], 0)
    return (hp, hs, stack(rows_p, 0), stack(rows_p, 1), stack(rows_p, 2), stack(rows_p, 3),
            stack(rows_s, 0), stack(rows_s, 1), stack(rows_s, 2), stack(rows_s, 3))
```

```python
import functools
import math
from typing import Callable, NamedTuple

import jax
import jax.numpy as jnp
import numpy as np
from jax import lax
from jax.experimental import pallas as pl
from jax.experimental.pallas import tpu as pltpu

CHUNK = 64
ROPE_THETA = 500000.0
RMS_EPS = 1e-6
DA_HEADS = 4
DA_QK_DIM = 64
DA_V_DIM = 2 * DA_QK_DIM
DA_ROT = DA_QK_DIM // 4
DA_COLS = DA_HEADS * DA_V_DIM
MLA_HEADS = 8
MLA_Q_LORA = 384
MLA_KV_LORA = 256
MLA_NOPE = 64
MLA_ROPE = 32
MLA_V_DIM = 64
MLA_WIDTH = MLA_HEADS * MLA_V_DIM
MLA_QK_DIM = MLA_NOPE + MLA_ROPE

LANES = 128
SUM_ROWS = 8
BF16_ROWS = 16
SUM_PAD = BF16_ROWS
MLA_VT_ROWS = 64 + SUM_PAD
MLA_QK_COLS = MLA_HEADS * LANES
MLA_FEAT = LANES - MLA_QK_DIM
LOG2E = math.log2(math.e)
NEG_BIG = -1e30
VMEM_LIMIT_BYTES = 56 * 1024 * 1024
HEAD_PAIR = 2
MLA_GROUP = 2
LOOP_STAGES = 2

F32 = jnp.float32
BF16 = jnp.bfloat16


def _rms(x, g):
    return x * lax.rsqrt(jnp.mean(x * x, axis=-1, keepdims=True) + RMS_EPS) * g


def _rope128(x, tab_ref, shift):
    return (x * tab_ref[0] + pltpu.roll(x, LANES - shift, 1) * tab_ref[1]
            + pltpu.roll(x, shift, 1) * tab_ref[2])


def _rope_tables(pos, rot, lane_of):
    half = rot // 2
    inv = jnp.float32(ROPE_THETA) ** (-jnp.arange(half, dtype=F32) * 2.0 / rot)
    ang = pos.astype(F32)[:, None] * inv
    cos, sin = jnp.cos(ang), jnp.sin(ang)
    idx = np.array([lane_of(j) for j in range(LANES)])
    lo = (idx >= 0) & (idx < half)
    hi = idx >= half
    src = np.where(idx >= 0, idx % half, 0)
    c = jnp.where(jnp.asarray(lo | hi), cos[:, src], 1.0)
    sa = jnp.where(jnp.asarray(lo), -sin[:, src], 0.0)
    sb = jnp.where(jnp.asarray(hi), sin[:, src], 0.0)
    return jnp.stack([c, sa, sb]).astype(F32)


def _da_lane(j):
    c = j % DA_QK_DIM
    return c if c < DA_ROT else -1


def _mla_lane(j):
    c = j - MLA_NOPE
    return c if 0 <= c < MLA_ROPE else -1


def _rope_rows(x, tab_ref, shift):
    up = jnp.concatenate([x[shift:], x[:shift]], axis=0)
    dn = jnp.concatenate([x[-shift:], x[:-shift]], axis=0)
    return x * tab_ref[0] + up * tab_ref[1] + dn * tab_ref[2]


def _inproj_kernel(x_ref, ng_ref, wn_ref, wt_ref, qg_ref, kvg_ref, wuqT_ref, tda_ref, tml_ref, tdaT_ref,
                   tmlT_ref, *refs, q_scale, qm_scale, expand):
    if expand:
        wuk_ref, wuv_ref = refs[:2]
        refs = refs[2:]
    qT_ref, k4_ref, v4_ref, kb_ref, vb_ref, lat_ref, kr_ref, kr32_ref, qmT_ref = refs[:9]
    o_v = DA_COLS
    o_ckv = 2 * DA_COLS
    o_kr = o_ckv + MLA_KV_LORA
    xn = _rms(x_ref[...], ng_ref[...])
    h = xn.astype(BF16)

    proj = jnp.dot(h, wn_ref[...], preferred_element_type=F32)
    for i in range(DA_HEADS):
        lo, hi = i * LANES, (i + 1) * LANES
        k = _rope128(proj[:, lo:hi], tda_ref, DA_ROT // 2)
        v = proj[:, o_v + lo:o_v + hi]
        k4_ref[0, :, i, :] = k
        v4_ref[0, :, i, :] = v
        kb_ref[:, lo:hi] = k.astype(BF16)
        vb_ref[:, lo:hi] = v.astype(BF16)
    lat = _rms(proj[:, o_ckv:o_ckv + MLA_KV_LORA], kvg_ref[...])
    lat_ref[...] = lat
    kr = _rope128(proj[:, o_kr:o_kr + LANES], tml_ref, MLA_ROPE // 2)
    kr_ref[...] = kr
    kr32_ref[...] = kr[:, MLA_NOPE:MLA_QK_DIM]
    if expand:
        _expand_tile(lat, kr, wuk_ref, wuv_ref, refs[9], refs[10], valid_rows=x_ref.shape[0])

    d = xn.shape[1]
    hT = jnp.concatenate([xn[:, c:c + LANES].T.astype(BF16) for c in range(0, d, LANES)], axis=0)
    projT = jnp.dot(wt_ref[...], hT, preferred_element_type=F32)
    for i in range(DA_HEADS):
        lo, hi = i * LANES, (i + 1) * LANES
        qT_ref[0, 0, lo:hi, :] = (_rope_rows(projT[lo:hi], tdaT_ref, DA_ROT // 2) * q_scale).astype(BF16)
    cqT = projT[DA_COLS:]
    qnT = (cqT * lax.rsqrt(jnp.mean(cqT * cqT, axis=0, keepdims=True) + RMS_EPS) * qg_ref[...]).astype(BF16)
    qmT = jnp.dot(wuqT_ref[...], qnT, preferred_element_type=F32)
    for i in range(MLA_HEADS):
        lo, hi = i * LANES, (i + 1) * LANES
        qmT_ref[0, 0, lo:hi, :] = (_rope_rows(qmT[lo:hi], tmlT_ref, MLA_ROPE // 2) * qm_scale).astype(BF16)


def _inproj(x, norm_g, w_nat, w_t, qg_col, kvg, wuqT, tda, tml, wuk, wuv, *, tm, expand):
    B, S, D = x.shape
    n = B * S
    nst = S // tm
    const = lambda i: (0, 0)
    tok = lambda i: (i, 0)
    feat_t = lambda i: (i // nst, i % nst, 0, 0)
    tab = lambda i: (0, i % nst, 0)
    tabT = lambda i: (0, 0, i % nst)
    in_specs = [
        pl.BlockSpec((tm, D), tok),
        pl.BlockSpec((1, D), const),
        pl.BlockSpec(w_nat.shape, const),
        pl.BlockSpec(w_t.shape, const),
        pl.BlockSpec((MLA_Q_LORA, 1), const),
        pl.BlockSpec((1, MLA_KV_LORA), const),
        pl.BlockSpec((MLA_QK_COLS, MLA_Q_LORA), const),
        pl.BlockSpec((3, tm, LANES), tab),
        pl.BlockSpec((3, tm, LANES), tab),
        pl.BlockSpec((3, LANES, tm), tabT),
        pl.BlockSpec((3, LANES, tm), tabT),
    ]
    out_specs = [
        pl.BlockSpec((1, 1, DA_COLS, tm), feat_t),
        pl.BlockSpec((1, tm, DA_HEADS, DA_V_DIM), feat_t),
        pl.BlockSpec((1, tm, DA_HEADS, DA_V_DIM), feat_t),
        pl.BlockSpec((tm, DA_COLS), tok),
        pl.BlockSpec((tm, DA_COLS), tok),
        pl.BlockSpec((tm, MLA_KV_LORA), tok),
        pl.BlockSpec((tm, LANES), tok),
        pl.BlockSpec((tm, MLA_ROPE), tok),
        pl.BlockSpec((1, 1, MLA_QK_COLS, tm), feat_t),
    ]
    out_shape = [
        jax.ShapeDtypeStruct((B, nst, DA_COLS, tm), BF16),
        jax.ShapeDtypeStruct((B, S, DA_HEADS, DA_V_DIM), F32),
        jax.ShapeDtypeStruct((B, S, DA_HEADS, DA_V_DIM), F32),
        jax.ShapeDtypeStruct((n, DA_COLS), BF16),
        jax.ShapeDtypeStruct((n, DA_COLS), BF16),
        jax.ShapeDtypeStruct((n, MLA_KV_LORA), F32),
        jax.ShapeDtypeStruct((n, LANES), F32),
        jax.ShapeDtypeStruct((n, MLA_ROPE), F32),
        jax.ShapeDtypeStruct((B, nst, MLA_QK_COLS, tm), BF16),
    ]
    args = [x.reshape(n, D), norm_g, w_nat, w_t, qg_col, kvg, wuqT, tda, tml,
            tda.transpose(0, 2, 1), tml.transpose(0, 2, 1)]
    if expand:
        assert _n_features(tm) <= MLA_FEAT
        in_specs += [pl.BlockSpec(wuk.shape, const), pl.BlockSpec(wuv.shape, const)]
        out_specs += [pl.BlockSpec((tm, MLA_QK_COLS), tok),
                      pl.BlockSpec((1, 1, MLA_HEADS * MLA_VT_ROWS, tm), feat_t)]
        out_shape += [jax.ShapeDtypeStruct((n, MLA_QK_COLS), BF16),
                      jax.ShapeDtypeStruct((B, nst, MLA_HEADS * MLA_VT_ROWS, tm), BF16)]
        args += [wuk, wuv]
    return pl.pallas_call(
        functools.partial(_inproj_kernel, q_scale=DA_QK_DIM ** -0.5 * LOG2E,
                          qm_scale=MLA_QK_DIM ** -0.5 * LOG2E, expand=expand),
        grid=(n // tm,),
        in_specs=in_specs,
        out_specs=out_specs,
        out_shape=out_shape,
        compiler_params=pltpu.CompilerParams(
            dimension_semantics=("arbitrary",), vmem_limit_bytes=VMEM_LIMIT_BYTES),
        name="inproj",
    )(*args)


def _n_features(tk):
    return tk // CHUNK + 1


def _key_features(tk, valid_rows, lane0, dtype):
    nf = _n_features(tk)
    row = lax.broadcasted_iota(jnp.int32, (tk, LANES), 0)
    f = lax.broadcasted_iota(jnp.int32, (tk, LANES), 1) - lane0
    chunk_hit = (f >= 0) & (f < nf - 1) & (lax.shift_right_logical(row, 6) == f)
    pad_hit = (f == nf - 1) & (row >= valid_rows)
    return jnp.where(chunk_hit | pad_hit, 1.0, 0.0).astype(dtype)


def _query_coeffs(nrows, ncols, tk, tq, q_rel, diag):
    nf = _n_features(tk)
    r = lax.broadcasted_iota(jnp.int32, (nrows, ncols), 0)
    col = lax.broadcasted_iota(jnp.int32, (nrows, ncols), 1)
    q_chunk = lax.shift_right_logical(q_rel + jnp.where(col >= tq, col - tq, col), 6)
    hidden = (r == nf - 1)
    if diag:
        hidden = hidden | ((r < nf - 1) & (r > q_chunk))
    return jnp.where(hidden, NEG_BIG, 0.0).astype(BF16)


def _expand_tile(lat, kr, wuk_ref, wuv_ref, km_ref, vT_ref, *, valid_rows):
    tk = lat.shape[0]
    c = lat.astype(BF16)
    kn = jnp.dot(c, wuk_ref[...], preferred_element_type=F32)
    tail = kr + _key_features(tk, valid_rows, MLA_QK_DIM, F32)
    for i in range(MLA_HEADS):
        lo, hi = i * LANES, (i + 1) * LANES
        km_ref[:, lo:hi] = (kn[:, lo:hi] + tail).astype(BF16)
    vm = jnp.dot(c, wuv_ref[...], preferred_element_type=F32)
    ones = _ones_rows(tk)
    for i in range(MLA_WIDTH // LANES):
        vt = vm[:, i * LANES:(i + 1) * LANES].T.astype(BF16)
        for hh in range(LANES // MLA_V_DIM):
            r0 = (i * (LANES // MLA_V_DIM) + hh) * MLA_VT_ROWS
            vT_ref[0, 0, r0:r0 + MLA_V_DIM, :] = vt[hh * MLA_V_DIM:(hh + 1) * MLA_V_DIM]
            vT_ref[0, 0, r0 + MLA_V_DIM:r0 + MLA_VT_ROWS, :] = ones


def _expand_kernel(c_ref, kr_ref, wuk_ref, wuv_ref, km_ref, vT_ref, *, valid_rows):
    _expand_tile(c_ref[0], kr_ref[0], wuk_ref, wuv_ref, km_ref.at[0], vT_ref, valid_rows=valid_rows)


def _expand(c, kr, wuk, wuv, *, tk, kv_len):
    B, T, _ = c.shape
    nkv = T // tk
    assert _n_features(tk) <= MLA_FEAT
    const = lambda b, j: (0, 0)
    tile = lambda b, j: (b, j, 0)
    return pl.pallas_call(
        functools.partial(_expand_kernel, valid_rows=kv_len - (nkv - 1) * tk),
        grid=(B, nkv),
        in_specs=[
            pl.BlockSpec((1, tk, MLA_KV_LORA), tile),
            pl.BlockSpec((1, tk, LANES), tile),
            pl.BlockSpec((MLA_KV_LORA, MLA_QK_COLS), const),
            pl.BlockSpec((MLA_KV_LORA, MLA_WIDTH), const),
        ],
        out_specs=[
            pl.BlockSpec((1, tk, MLA_QK_COLS), tile),
            pl.BlockSpec((1, 1, MLA_HEADS * MLA_VT_ROWS, tk), lambda b, j: (b, j, 0, 0)),
        ],
        out_shape=[
            jax.ShapeDtypeStruct((B, T, MLA_QK_COLS), BF16),
            jax.ShapeDtypeStruct((B, nkv, MLA_HEADS * MLA_VT_ROWS, tk), BF16),
        ],
        compiler_params=pltpu.CompilerParams(
            dimension_semantics=("arbitrary", "arbitrary"), vmem_limit_bytes=VMEM_LIMIT_BYTES),
        name="kv_expand",
    )(c, kr, wuk, wuv)


def _check_tiling(*, nq, tq, tk, nkv, q_off, kv_len):
    single = nq == 1 and nkv == 1
    causal = q_off == 0 and tq == tk and nq == nkv and kv_len == nkv * tk and (nq * (nq + 1) // 2) % 2 == 0
    assert single or causal, (nq, tq, tk, nkv, q_off, kv_len)


class _Chain(NamedTuple):
    load_k: Callable
    load_vT: Callable
    load_q: Callable
    s: object
    mx: object
    p: object
    a: object
    m: object
    acc: object


def _flash_flat(chains, nq, finalize):
    for c in chains:
        tk = c.s.shape[1]
        c.m[...] = jnp.full(c.m.shape, NEG_BIG, F32)
        c.acc[...] = jnp.zeros(c.acc.shape, F32)

    def advance(u):
        qi, j = u
        last = j == qi
        return jnp.where(last, qi + 1, qi), jnp.where(last, 0, j + 1)

    def scores(u, slot):
        qi, j = u
        for c in chains:
            s = jnp.dot(c.load_k(j), c.load_q(qi, j == qi), preferred_element_type=F32)
            c.s[slot] = s
            c.mx[slot] = jnp.max(s, axis=0, keepdims=True)

    def softmax(u, slot):
        qi, j = u
        for c in chains:
            m_prev = jnp.where(j == 0, NEG_BIG, c.m[...])
            m_new = jnp.maximum(m_prev, c.mx[slot])
            c.m[...] = m_new
            c.a[slot] = jnp.exp2(m_prev - m_new)
            for r in range(0, tk, BF16_ROWS):
                x = (c.s[slot, r:r + BF16_ROWS, :] - m_new).astype(BF16)
                c.p[slot, r:r + BF16_ROWS, :] = jnp.exp2(x)

    def values(u, slot):
        qi, j = u
        par = lax.rem(qi, 2)
        for c in chains:
            pv = jnp.dot(c.load_vT(j), c.p[slot], preferred_element_type=F32)
            c.acc[par] = c.a[slot] * c.acc[par] + pv

    def stages(ua, count):
        units = [ua]
        for _ in range(count + 1):
            units.append(advance(units[-1]))
        for i in range(count):
            cur, nxt = (i + 1) % 2, i % 2
            scores(units[i + 2], nxt)
            softmax(units[i + 1], cur)
            values(units[i], nxt)
        for i in range(0, count, 2):
            done_a = units[i][1] == units[i][0]
            done_b = units[i + 1][1] == units[i + 1][0]

            @pl.when(jnp.logical_or(done_a, done_b))
            def _():
                finalize(jnp.where(done_a, units[i][0], units[i + 1][0]))

        return units[count]

    zero = jnp.int32(0)
    u0 = (zero, zero)
    n_units = nq * (nq + 1) // 2
    if n_units == 1:
        scores(u0, 0)
        softmax(u0, 0)
        values(u0, 0)
        finalize(0)
        return

    scores(u0, 0)
    scores(advance(u0), 1)
    softmax(u0, 0)

    ua = lax.fori_loop(0, (n_units - 2) // LOOP_STAGES, lambda t, ua: stages(ua, LOOP_STAGES), u0)
    rest = (n_units - 2) % LOOP_STAGES
    if rest:
        ua = stages(ua, rest)
    ub = advance(ua)
    softmax(ub, 1)
    values(ua, 0)
    values(ub, 1)
    finalize(nq - 1)


def _ones_rows(tk):
    row = lax.broadcasted_iota(jnp.int32, (SUM_PAD, tk), 0)
    return jnp.where(row == 0, 1.0, 0.0).astype(BF16)


def _da_attn_kernel(lam_ref, hg_ref, qT_ref, k_ref, v_ref, o_ref,
                    vT_ref, qs_ref, feat_ref, gv_ref, s_ref, mx_ref, p_ref, a_ref, m_ref, acc_ref,
                    *, nq, tq, tk, nkv, q_off, kv_len, lam_init):
    valid_rows = kv_len - (nkv - 1) * tk
    feat_ref[...] = _key_features(tk, valid_rows, 0, BF16)
    q_rel = q_off if nkv == 1 else 0
    gv_ref[0] = _query_coeffs(LANES, 2 * tq, tk, tq, q_rel, diag=False)
    gv_ref[1] = _query_coeffs(LANES, 2 * tq, tk, tq, q_rel, diag=True)

    row = lax.broadcasted_iota(jnp.int32, (LANES, tq), 0)
    zero = jnp.zeros((LANES, tq), BF16)
    chains = []
    for hh in range(HEAD_PAIR):
        cols = slice(hh * LANES, (hh + 1) * LANES)
        for c in range(nkv):
            vT_ref[hh, c, :DA_V_DIM] = v_ref[0, c * tk:(c + 1) * tk, cols].astype(F32).T.astype(BF16)
            vT_ref[hh, c, DA_V_DIM:] = _ones_rows(tk)
        for qi in range(nq):
            qT = qT_ref[0, qi, cols, :]
            qs_ref[hh, qi, :, :tq] = jnp.where(row < DA_QK_DIM, qT, zero)
            qs_ref[hh, qi, :, tq:] = jnp.where(row >= DA_QK_DIM, qT, zero)
        chains.append(_Chain(
            load_k=lambda j, cols=cols: jnp.concatenate(
                [k_ref[0, pl.ds(pl.multiple_of(j * tk, tk), tk), cols], feat_ref[...]], axis=1),
            load_vT=lambda j, hh=hh: vT_ref[hh, j],
            load_q=lambda qi, diag, hh=hh: jnp.concatenate(
                [qs_ref[hh, qi], gv_ref[diag.astype(jnp.int32)]], axis=0),
            s=s_ref.at[hh], mx=mx_ref.at[hh], p=p_ref.at[hh], a=a_ref.at[hh], m=m_ref.at[hh],
            acc=acc_ref.at[hh]))

    lf = lam_ref[...]
    lam = (jnp.exp(jnp.sum(lf[0:1] * lf[1:2], axis=-1, keepdims=True))
           - jnp.exp(jnp.sum(lf[2:3] * lf[3:4], axis=-1, keepdims=True)) + lam_init)

    def finalize(qi):
        par = qi % 2
        row0 = qi * tq if isinstance(qi, int) else pl.multiple_of(qi * tq, tq)
        for hh in range(HEAD_PAIR):
            acc = acc_ref[hh, par]
            o = acc[:DA_V_DIM] * (1.0 / acc[DA_V_DIM:DA_V_DIM + 1])
            o = o[:, :tq] - lam * o[:, tq:]
            y = o * lax.rsqrt(jnp.mean(o * o, axis=0, keepdims=True) + RMS_EPS) * hg_ref[...] * (1.0 - lam_init)
            o_ref[0, pl.ds(row0, tq), hh * LANES:(hh + 1) * LANES] = y.T.astype(BF16)

    _flash_flat(chains, nq, finalize)


def _da_attention(lam, hg, qT, k, v, *, tk, q_off, kv_len, lam_init):
    B, nq, _, tq = qT.shape
    T = k.shape[1]
    nkv = T // tk
    _check_tiling(nq=nq, tq=tq, tk=tk, nkv=nkv, q_off=q_off, kv_len=kv_len)
    assert _n_features(tk) <= LANES
    pair = HEAD_PAIR * LANES
    head_kv = lambda b, h: (b, 0, h)
    return pl.pallas_call(
        functools.partial(_da_attn_kernel, nq=nq, tq=tq, tk=tk, nkv=nkv, q_off=q_off, kv_len=kv_len,
                          lam_init=lam_init),
        grid=(B, DA_HEADS // HEAD_PAIR),
        in_specs=[
            pl.BlockSpec(lam.shape, lambda b, h: (0, 0)),
            pl.BlockSpec((DA_V_DIM, 1), lambda b, h: (0, 0)),
            pl.BlockSpec((1, nq, pair, tq), lambda b, h: (b, 0, h, 0)),
            pl.BlockSpec((1, T, pair), head_kv),
            pl.BlockSpec((1, T, pair), head_kv),
        ],
        out_specs=pl.BlockSpec((1, nq * tq, pair), head_kv),
        out_shape=jax.ShapeDtypeStruct((B, nq * tq, DA_COLS), BF16),
        scratch_shapes=[
            pltpu.VMEM((HEAD_PAIR, nkv, DA_V_DIM + SUM_PAD, tk), BF16),
            pltpu.VMEM((HEAD_PAIR, nq, LANES, 2 * tq), BF16),
            pltpu.VMEM((tk, LANES), BF16),
            pltpu.VMEM((2, LANES, 2 * tq), BF16),
            pltpu.VMEM((HEAD_PAIR, 2, tk, 2 * tq), F32),
            pltpu.VMEM((HEAD_PAIR, 2, 1, 2 * tq), F32),
            pltpu.VMEM((HEAD_PAIR, 2, tk, 2 * tq), BF16),
            pltpu.VMEM((HEAD_PAIR, 2, 1, 2 * tq), F32),
            pltpu.VMEM((HEAD_PAIR, 1, 2 * tq), F32),
            pltpu.VMEM((HEAD_PAIR, 2, DA_V_DIM + SUM_PAD, 2 * tq), F32),
        ],
        compiler_params=pltpu.CompilerParams(
            dimension_semantics=("arbitrary", "arbitrary"), vmem_limit_bytes=VMEM_LIMIT_BYTES),
        name="da_attention",
    )(lam, hg, qT, k, v)


def _mla_attn_kernel(qT_ref, k_ref, vT_ref, o_ref, gv_ref, s_ref, mx_ref, p_ref, a_ref, m_ref, acc_ref,
                     *, nq, tq, tk, nkv, q_off):
    q_rel = q_off if nkv == 1 else 0
    gv_ref[0] = _query_coeffs(MLA_FEAT, tq, tk, tq, q_rel, diag=False)
    gv_ref[1] = _query_coeffs(MLA_FEAT, tq, tk, tq, q_rel, diag=True)
    chains = []
    for hh in range(MLA_GROUP):
        rows = slice(hh * MLA_VT_ROWS, (hh + 1) * MLA_VT_ROWS)
        cols = slice(hh * LANES, (hh + 1) * LANES)
        qrows = slice(hh * LANES, hh * LANES + MLA_QK_DIM)
        chains.append(_Chain(
            load_k=lambda j, cols=cols: k_ref[0, pl.ds(pl.multiple_of(j * tk, tk), tk), cols],
            load_vT=lambda j, rows=rows: vT_ref[0, j, rows, :],
            load_q=lambda qi, diag, qrows=qrows: jnp.concatenate(
                [qT_ref[0, qi, qrows, :], gv_ref[diag.astype(jnp.int32)]], axis=0),
            s=s_ref.at[hh], mx=mx_ref.at[hh], p=p_ref.at[hh], a=a_ref.at[hh], m=m_ref.at[hh],
            acc=acc_ref.at[:, rows]))

    def finalize(qi):
        par = qi % 2
        row0 = qi * tq if isinstance(qi, int) else pl.multiple_of(qi * tq, tq)
        o = acc_ref[par]
        halves = [o[hh * MLA_VT_ROWS:hh * MLA_VT_ROWS + MLA_V_DIM]
                  * (1.0 / o[hh * MLA_VT_ROWS + MLA_V_DIM:hh * MLA_VT_ROWS + MLA_V_DIM + 1])
                  for hh in range(MLA_GROUP)]
        for i in range(0, MLA_GROUP, 2):
            o_ref[0, pl.ds(row0, tq), i * MLA_V_DIM:(i + 2) * MLA_V_DIM] = (
                jnp.concatenate(halves[i:i + 2], axis=0).T.astype(BF16))

    _flash_flat(chains, nq, finalize)


def _mla_attention(qmT, km, vT, *, q_off, kv_len):
    B, nq, _, tq = qmT.shape
    nkv, tk = vT.shape[1], vT.shape[3]
    T = km.shape[1]
    _check_tiling(nq=nq, tq=tq, tk=tk, nkv=nkv, q_off=q_off, kv_len=kv_len)
    ncols = tq
    return pl.pallas_call(
        functools.partial(_mla_attn_kernel, nq=nq, tq=tq, tk=tk, nkv=nkv, q_off=q_off),
        grid=(B, MLA_HEADS // MLA_GROUP),
        in_specs=[
            pl.BlockSpec((1, nq, MLA_GROUP * LANES, tq), lambda b, h: (b, 0, h, 0)),
            pl.BlockSpec((1, T, MLA_GROUP * LANES), lambda b, h: (b, 0, h)),
            pl.BlockSpec((1, nkv, MLA_GROUP * MLA_VT_ROWS, tk), lambda b, h: (b, 0, h, 0)),
        ],
        out_specs=pl.BlockSpec((1, nq * tq, MLA_GROUP * MLA_V_DIM), lambda b, h: (b, 0, h)),
        out_shape=jax.ShapeDtypeStruct((B, nq * tq, MLA_WIDTH), BF16),
        scratch_shapes=[
            pltpu.VMEM((2, MLA_FEAT, ncols), BF16),
            pltpu.VMEM((MLA_GROUP, 2, tk, ncols), F32),
            pltpu.VMEM((MLA_GROUP, 2, 1, ncols), F32),
            pltpu.VMEM((MLA_GROUP, 2, tk, ncols), BF16),
            pltpu.VMEM((MLA_GROUP, 2, 1, ncols), F32),
            pltpu.VMEM((MLA_GROUP, 1, ncols), F32),
            pltpu.VMEM((2, MLA_GROUP * MLA_VT_ROWS, ncols), F32),
        ],
        compiler_params=pltpu.CompilerParams(
            dimension_semantics=("arbitrary", "arbitrary"), vmem_limit_bytes=VMEM_LIMIT_BYTES),
        name="mla_attention",
    )(qmT, km, vT)


def _epilogue_kernel(x_ref, oa_ref, ob_ref, ng_ref, wzg_ref, gb_ref, wa_ref, wb_ref, wo_ref, fg_ref,
                     y_ref, *, final_norm):
    x = x_ref[...]
    d = x.shape[1]
    h = _rms(x, ng_ref[...]).astype(BF16)
    zg = jnp.dot(h, wzg_ref[...], preferred_element_type=F32)
    ga = (oa_ref[...].astype(F32) * jax.nn.silu(zg[:, :DA_COLS])).astype(BF16)
    gb = (ob_ref[...].astype(F32) * jax.nn.silu(zg[:, DA_COLS:DA_COLS + MLA_WIDTH])).astype(BF16)
    ya = jnp.dot(ga, wa_ref[...], preferred_element_type=F32)
    yb = jnp.dot(gb, wb_ref[...], preferred_element_type=F32)
    g = jax.nn.sigmoid(zg[:, DA_COLS + MLA_WIDTH:] + gb_ref[...])
    m = (g[:, :d] * ya + g[:, d:] * yb).astype(BF16)
    out = x + jnp.dot(m, wo_ref[...], preferred_element_type=F32)
    y_ref[...] = _rms(out, fg_ref[...]) if final_norm else out


def _epilogue(x, oa, ob, norm_g, wzg, gate_b, wa, wb, wo, final_g, *, tm, final_norm):
    n, D = x.shape
    const = lambda i: (0, 0)
    tok = lambda i: (i, 0)
    return pl.pallas_call(
        functools.partial(_epilogue_kernel, final_norm=final_norm),
        grid=(n // tm,),
        in_specs=[
            pl.BlockSpec((tm, D), tok),
            pl.BlockSpec((tm, DA_COLS), tok),
            pl.BlockSpec((tm, MLA_WIDTH), tok),
            pl.BlockSpec((1, D), const),
            pl.BlockSpec(wzg.shape, const),
            pl.BlockSpec((1, 2 * D), const),
            pl.BlockSpec(wa.shape, const),
            pl.BlockSpec(wb.shape, const),
            pl.BlockSpec(wo.shape, const),
            pl.BlockSpec((1, D), const),
        ],
        out_specs=pl.BlockSpec((tm, D), tok),
        out_shape=jax.ShapeDtypeStruct((n, D), F32),
        compiler_params=pltpu.CompilerParams(
            dimension_semantics=("arbitrary",), vmem_limit_bytes=VMEM_LIMIT_BYTES),
        name="epilogue",
    )(x, oa, ob, norm_g, wzg, gate_b, wa, wb, wo, final_g)


def _tiles(S):
    t = min(512, S)
    assert S % t == 0 and t % LANES == 0
    return t


def _prep_weights(w_in, mla_w_uq, mla_w_uk):
    D = w_in.shape[0]
    sizes = (DA_COLS, DA_COLS, DA_COLS, DA_COLS, MLA_Q_LORA, MLA_KV_LORA, MLA_ROPE, MLA_WIDTH, 2 * D)
    assert w_in.shape[1] == sum(sizes)
    w_q, w_k, w_v, w_za, w_cq, w_ckv, w_kr, w_zb, w_g = jnp.split(w_in, np.cumsum(sizes)[:-1].tolist(), axis=1)
    w_kr = jnp.pad(w_kr, ((0, 0), (MLA_NOPE, LANES - MLA_QK_DIM)))
    w_nat = jnp.concatenate([w_k, w_v, w_ckv, w_kr], axis=1).astype(BF16)
    w_t = jnp.concatenate([w_q, w_cq], axis=1).T.astype(BF16)
    w_zg = jnp.concatenate([w_za, w_zb, w_g], axis=1).astype(BF16)
    pad_heads = lambda w, dh: jnp.pad(
        w.reshape(w.shape[0], MLA_HEADS, dh), ((0, 0), (0, 0), (0, LANES - dh))
    ).reshape(w.shape[0], MLA_QK_COLS).astype(BF16)
    return w_nat, w_t, w_zg, pad_heads(mla_w_uq, MLA_QK_DIM).T, pad_heads(mla_w_uk, MLA_NOPE)


class _LayerWeights(NamedTuple):
    norm_g: jax.Array
    w_nat: jax.Array
    w_t: jax.Array
    w_zg: jax.Array
    gate_b: jax.Array
    da_lambda: jax.Array
    da_head_norm_g: jax.Array
    mla_q_norm_g: jax.Array
    mla_kv_norm_g: jax.Array
    wuqT: jax.Array
    wuk: jax.Array
    wuv: jax.Array
    wa: jax.Array
    wb: jax.Array
    wo: jax.Array
    final_norm_g: jax.Array


def _layer(x, past, lam_init, final_norm, w):
    B, S, D = x.shape
    past_len = 0 if past is None else past[0].shape[1]
    Sp = -(-S // LANES) * LANES
    xp = jnp.pad(x, ((0, 0), (0, Sp - S), (0, 0))) if Sp != S else x
    tm = _tiles(Sp)
    pos = past_len + jnp.arange(Sp, dtype=jnp.int32)
    tda = _rope_tables(pos, DA_ROT, _da_lane)
    tml = _rope_tables(pos, MLA_ROPE, _mla_lane)

    outs = _inproj(xp, w.norm_g, w.w_nat, w.w_t, w.mla_q_norm_g, w.mla_kv_norm_g, w.wuqT, tda, tml, w.wuk, w.wuv,
                   tm=tm, expand=past is None)
    qT, k_new, v_new, k_b, v_b, lat_new, kr_new, kr32, qmT = outs[:9]
    lat_new = lat_new.reshape(B, Sp, MLA_KV_LORA)

    kv_len = past_len + S
    if past is None:
        k_all, v_all = k_b.reshape(B, Sp, DA_COLS), v_b.reshape(B, Sp, DA_COLS)
        km, vmT = outs[9].reshape(B, Sp, MLA_QK_COLS), outs[10]
        tk = tm
    else:
        pk, pv, pc, pr = past
        tk = -(-kv_len // LANES) * LANES
        cat = lambda old, new: jnp.pad(jnp.concatenate([old, new[:, :S]], axis=1),
                                       ((0, 0), (0, tk - kv_len), (0, 0)))
        k_all = cat(pk.reshape(B, past_len, DA_COLS), k_new.reshape(B, Sp, DA_COLS)).astype(BF16)
        v_all = cat(pv.reshape(B, past_len, DA_COLS), v_new.reshape(B, Sp, DA_COLS)).astype(BF16)
        c_all = cat(pc, lat_new)
        kr_all = cat(jnp.pad(pr, ((0, 0), (0, 0), (MLA_NOPE, LANES - MLA_QK_DIM))), kr_new.reshape(B, Sp, LANES))
        km, vmT = _expand(c_all, kr_all, w.wuk, w.wuv, tk=tk, kv_len=kv_len)

    o_a = _da_attention(w.da_lambda, w.da_head_norm_g, qT, k_all, v_all,
                        tk=tk, q_off=past_len, kv_len=kv_len, lam_init=lam_init)
    o_b = _mla_attention(qmT, km, vmT, q_off=past_len, kv_len=kv_len)

    n = B * Sp
    y = _epilogue(xp.reshape(n, D), o_a.reshape(n, DA_COLS), o_b.reshape(n, MLA_WIDTH), w.norm_g, w.w_zg,
                  w.gate_b, w.wa, w.wb, w.wo, w.final_norm_g, tm=_tiles(n), final_norm=final_norm)
    y = y.reshape(B, Sp, D)[:, :S]
    new = (k_new[:, :S], v_new[:, :S], lat_new[:, :S], kr32.reshape(B, Sp, MLA_ROPE)[:, :S])
    return y, new


def kernel(x_prompt, x_sample, cache_da_k, cache_da_v, cache_mla_latent, cache_mla_krope, norm_g, w_in, gate_b, da_lambda, da_head_norm_g, mla_q_norm_g, mla_kv_norm_g, mla_w_uq, mla_w_uk, mla_w_uv, w_branch_a, w_branch_b, w_out, final_norm_g):
    depth = w_in.shape[0]
    at = lambda a, l: a.reshape(a.shape[1:]) if depth == 1 else a[l]
    row = lambda v: v.reshape(1, -1)
    hp, hs = x_prompt, x_sample
    rows_p, rows_s = [], []
    for l in range(depth):
        lam_init = 0.8 - 0.6 * math.exp(-0.3 * l)
        last = l == depth - 1
        w_nat, w_t, w_zg, wuqT, wuk = _prep_weights(at(w_in, l), at(mla_w_uq, l), at(mla_w_uk, l))
        w = _LayerWeights(
            norm_g=row(at(norm_g, l)), w_nat=w_nat, w_t=w_t, w_zg=w_zg, gate_b=row(at(gate_b, l)),
            da_lambda=at(da_lambda, l), da_head_norm_g=at(da_head_norm_g, l).reshape(-1, 1),
            mla_q_norm_g=at(mla_q_norm_g, l).reshape(-1, 1), mla_kv_norm_g=row(at(mla_kv_norm_g, l)),
            wuqT=wuqT, wuk=wuk, wuv=at(mla_w_uv, l).astype(BF16), wa=at(w_branch_a, l).astype(BF16),
            wb=at(w_branch_b, l).astype(BF16), wo=at(w_out, l).astype(BF16), final_norm_g=row(final_norm_g))
        hp, new_p = _layer(hp, None, lam_init, last, w)
        past = (at(cache_da_k, l), at(cache_da_v, l), at(cache_mla_latent, l), at(cache_mla_krope, l))
        hs, new_s = _layer(hs, past, lam_init, last, w)
        rows_p.append(new_p)
        rows_s.append(new_s)
    stack = lambda rows, i: jnp.stack([r[i] for r in rows], 0)
    return (hp, hs, stack(rows_p, 0), stack(rows_p, 1), stack(rows_p, 2), stack(rows_p, 3),
            stack(rows_s, 0), stack(rows_s, 1), stack(rows_s, 2), stack(rows_s, 3))
```

```python
import functools
import math
from typing import Callable, NamedTuple

import jax
import jax.numpy as jnp
import numpy as np
from jax import lax
from jax.experimental import pallas as pl
from jax.experimental.pallas import tpu as pltpu

CHUNK = 64
ROPE_THETA = 500000.0
RMS_EPS = 1e-6
DA_HEADS = 4
DA_QK_DIM = 64
DA_V_DIM = 2 * DA_QK_DIM
DA_ROT = DA_QK_DIM // 4
DA_COLS = DA_HEADS * DA_V_DIM
MLA_HEADS = 8
MLA_Q_LORA = 384
MLA_KV_LORA = 256
MLA_NOPE = 64
MLA_ROPE = 32
MLA_V_DIM = 64
MLA_WIDTH = MLA_HEADS * MLA_V_DIM
MLA_QK_DIM = MLA_NOPE + MLA_ROPE

LANES = 128
SUM_ROWS = 8
BF16_ROWS = 16
SUM_PAD = BF16_ROWS
MLA_VT_ROWS = 64 + SUM_PAD
MLA_QK_COLS = MLA_HEADS * LANES
MLA_FEAT = LANES - MLA_QK_DIM
LOG2E = math.log2(math.e)
NEG_BIG = -1e30
VMEM_LIMIT_BYTES = 56 * 1024 * 1024
HEAD_PAIR = 2
MLA_GROUP = 2
LOOP_STAGES = 2

F32 = jnp.float32
BF16 = jnp.bfloat16


def _rms(x, g):
    return x * lax.rsqrt(jnp.mean(x * x, axis=-1, keepdims=True) + RMS_EPS) * g


def _rope128(x, tab_ref, shift):
    return (x * tab_ref[0] + pltpu.roll(x, LANES - shift, 1) * tab_ref[1]
            + pltpu.roll(x, shift, 1) * tab_ref[2])


def _rope_tables(pos, rot, lane_of):
    half = rot // 2
    inv = jnp.float32(ROPE_THETA) ** (-jnp.arange(half, dtype=F32) * 2.0 / rot)
    ang = pos.astype(F32)[:, None] * inv
    cos, sin = jnp.cos(ang), jnp.sin(ang)
    idx = np.array([lane_of(j) for j in range(LANES)])
    lo = (idx >= 0) & (idx < half)
    hi = idx >= half
    src = np.where(idx >= 0, idx % half, 0)
    c = jnp.where(jnp.asarray(lo | hi), cos[:, src], 1.0)
    sa = jnp.where(jnp.asarray(lo), -sin[:, src], 0.0)
    sb = jnp.where(jnp.asarray(hi), sin[:, src], 0.0)
    return jnp.stack([c, sa, sb]).astype(F32)


def _da_lane(j):
    c = j % DA_QK_DIM
    return c if c < DA_ROT else -1


def _mla_lane(j):
    c = j - MLA_NOPE
    return c if 0 <= c < MLA_ROPE else -1


def _rope_rows(x, tab_ref, shift):
    up = jnp.concatenate([x[shift:], x[:shift]], axis=0)
    dn = jnp.concatenate([x[-shift:], x[:-shift]], axis=0)
    return x * tab_ref[0] + up * tab_ref[1] + dn * tab_ref[2]


def _inproj_kernel(x_ref, ng_ref, wn_ref, wt_ref, qg_ref, kvg_ref, wuqT_ref, tda_ref, tml_ref, tdaT_ref,
                   tmlT_ref, *refs, q_scale, qm_scale, expand):
    if expand:
        wuk_ref, wuv_ref = refs[:2]
        refs = refs[2:]
    qT_ref, k4_ref, v4_ref, kb_ref, vb_ref, lat_ref, kr_ref, kr32_ref, qmT_ref = refs[:9]
    o_v = DA_COLS
    o_ckv = 2 * DA_COLS
    o_kr = o_ckv + MLA_KV_LORA
    xn = _rms(x_ref[...], ng_ref[...])
    h = xn.astype(BF16)

    proj = jnp.dot(h, wn_ref[...], preferred_element_type=F32)
    for i in range(DA_HEADS):
        lo, hi = i * LANES, (i + 1) * LANES
        k = _rope128(proj[:, lo:hi], tda_ref, DA_ROT // 2)
        v = proj[:, o_v + lo:o_v + hi]
        k4_ref[0, :, i, :] = k
        v4_ref[0, :, i, :] = v
        kb_ref[:, lo:hi] = k.astype(BF16)
        vb_ref[:, lo:hi] = v.astype(BF16)
    lat = _rms(proj[:, o_ckv:o_ckv + MLA_KV_LORA], kvg_ref[...])
    lat_ref[...] = lat
    kr = _rope128(proj[:, o_kr:o_kr + LANES], tml_ref, MLA_ROPE // 2)
    kr_ref[...] = kr
    kr32_ref[...] = kr[:, MLA_NOPE:MLA_QK_DIM]
    if expand:
        _expand_tile(lat, kr, wuk_ref, wuv_ref, refs[9], refs[10], valid_rows=x_ref.shape[0])

    d = xn.shape[1]
    hT = jnp.concatenate([xn[:, c:c + LANES].T.astype(BF16) for c in range(0, d, LANES)], axis=0)
    projT = jnp.dot(wt_ref[...], hT, preferred_element_type=F32)
    for i in range(DA_HEADS):
        lo, hi = i * LANES, (i + 1) * LANES
        qT_ref[0, 0, lo:hi, :] = (_rope_rows(projT[lo:hi], tdaT_ref, DA_ROT // 2) * q_scale).astype(BF16)
    cqT = projT[DA_COLS:]
    qnT = (cqT * lax.rsqrt(jnp.mean(cqT * cqT, axis=0, keepdims=True) + RMS_EPS) * qg_ref[...]).astype(BF16)
    qmT = jnp.dot(wuqT_ref[...], qnT, preferred_element_type=F32)
    for i in range(MLA_HEADS):
        lo, hi = i * LANES, (i + 1) * LANES
        qmT_ref[0, 0, lo:hi, :] = (_rope_rows(qmT[lo:hi], tmlT_ref, MLA_ROPE // 2) * qm_scale).astype(BF16)


def _inproj(x, norm_g, w_nat, w_t, qg_col, kvg, wuqT, tda, tml, wuk, wuv, *, tm, expand):
    B, S, D = x.shape
    n = B * S
    nst = S // tm
    const = lambda i: (0, 0)
    tok = lambda i: (i, 0)
    feat_t = lambda i: (i // nst, i % nst, 0, 0)
    tab = lambda i: (0, i % nst, 0)
    tabT = lambda i: (0, 0, i % nst)
    in_specs = [
        pl.BlockSpec((tm, D), tok),
        pl.BlockSpec((1, D), const),
        pl.BlockSpec(w_nat.shape, const),
        pl.BlockSpec(w_t.shape, const),
        pl.BlockSpec((MLA_Q_LORA, 1), const),
        pl.BlockSpec((1, MLA_KV_LORA), const),
        pl.BlockSpec((MLA_QK_COLS, MLA_Q_LORA), const),
        pl.BlockSpec((3, tm, LANES), tab),
        pl.BlockSpec((3, tm, LANES), tab),
        pl.BlockSpec((3, LANES, tm), tabT),
        pl.BlockSpec((3, LANES, tm), tabT),
    ]
    out_specs = [
        pl.BlockSpec((1, 1, DA_COLS, tm), feat_t),
        pl.BlockSpec((1, tm, DA_HEADS, DA_V_DIM), feat_t),
        pl.BlockSpec((1, tm, DA_HEADS, DA_V_DIM), feat_t),
        pl.BlockSpec((tm, DA_COLS), tok),
        pl.BlockSpec((tm, DA_COLS), tok),
        pl.BlockSpec((tm, MLA_KV_LORA), tok),
        pl.BlockSpec((tm, LANES), tok),
        pl.BlockSpec((tm, MLA_ROPE), tok),
        pl.BlockSpec((1, 1, MLA_QK_COLS, tm), feat_t),
    ]
    out_shape = [
        jax.ShapeDtypeStruct((B, nst, DA_COLS, tm), BF16),
        jax.ShapeDtypeStruct((B, S, DA_HEADS, DA_V_DIM), F32),
        jax.ShapeDtypeStruct((B, S, DA_HEADS, DA_V_DIM), F32),
        jax.ShapeDtypeStruct((n, DA_COLS), BF16),
        jax.ShapeDtypeStruct((n, DA_COLS), BF16),
        jax.ShapeDtypeStruct((n, MLA_KV_LORA), F32),
        jax.ShapeDtypeStruct((n, LANES), F32),
        jax.ShapeDtypeStruct((n, MLA_ROPE), F32),
        jax.ShapeDtypeStruct((B, nst, MLA_QK_COLS, tm), BF16),
    ]
    args = [x.reshape(n, D), norm_g, w_nat, w_t, qg_col, kvg, wuqT, tda, tml,
            tda.transpose(0, 2, 1), tml.transpose(0, 2, 1)]
    if expand:
        assert _n_features(tm) <= MLA_FEAT
        in_specs += [pl.BlockSpec(wuk.shape, const), pl.BlockSpec(wuv.shape, const)]
        out_specs += [pl.BlockSpec((tm, MLA_QK_COLS), tok),
                      pl.BlockSpec((1, 1, MLA_HEADS * MLA_VT_ROWS, tm), feat_t)]
        out_shape += [jax.ShapeDtypeStruct((n, MLA_QK_COLS), BF16),
                      jax.ShapeDtypeStruct((B, nst, MLA_HEADS * MLA_VT_ROWS, tm), BF16)]
        args += [wuk, wuv]
    return pl.pallas_call(
        functools.partial(_inproj_kernel, q_scale=DA_QK_DIM ** -0.5 * LOG2E,
                          qm_scale=MLA_QK_DIM ** -0.5 * LOG2E, expand=expand),
        grid=(n // tm,),
        in_specs=in_specs,
        out_specs=out_specs,
        out_shape=out_shape,
        compiler_params=pltpu.CompilerParams(
            dimension_semantics=("arbitrary",), vmem_limit_bytes=VMEM_LIMIT_BYTES),
        name="inproj",
    )(*args)


def _n_features(tk):
    return tk // CHUNK + 1


def _key_features(tk, valid_rows, lane0, dtype):
    nf = _n_features(tk)
    row = lax.broadcasted_iota(jnp.int32, (tk, LANES), 0)
    f = lax.broadcasted_iota(jnp.int32, (tk, LANES), 1) - lane0
    chunk_hit = (f >= 0) & (f < nf - 1) & (lax.shift_right_logical(row, 6) == f)
    pad_hit = (f == nf - 1) & (row >= valid_rows)
    return jnp.where(chunk_hit | pad_hit, 1.0, 0.0).astype(dtype)


def _query_coeffs(nrows, ncols, tk, tq, q_rel, diag):
    nf = _n_features(tk)
    r = lax.broadcasted_iota(jnp.int32, (nrows, ncols), 0)
    col = lax.broadcasted_iota(jnp.int32, (nrows, ncols), 1)
    q_chunk = lax.shift_right_logical(q_rel + jnp.where(col >= tq, col - tq, col), 6)
    hidden = (r == nf - 1)
    if diag:
        hidden = hidden | ((r < nf - 1) & (r > q_chunk))
    return jnp.where(hidden, NEG_BIG, 0.0).astype(BF16)


def _expand_tile(lat, kr, wuk_ref, wuv_ref, km_ref, vT_ref, *, valid_rows):
    tk = lat.shape[0]
    c = lat.astype(BF16)
    kn = jnp.dot(c, wuk_ref[...], preferred_element_type=F32)
    tail = kr + _key_features(tk, valid_rows, MLA_QK_DIM, F32)
    for i in range(MLA_HEADS):
        lo, hi = i * LANES, (i + 1) * LANES
        km_ref[:, lo:hi] = (kn[:, lo:hi] + tail).astype(BF16)
    vm = jnp.dot(c, wuv_ref[...], preferred_element_type=F32)
    ones = _ones_rows(tk)
    for i in range(MLA_WIDTH // LANES):
        vt = vm[:, i * LANES:(i + 1) * LANES].T.astype(BF16)
        for hh in range(LANES // MLA_V_DIM):
            r0 = (i * (LANES // MLA_V_DIM) + hh) * MLA_VT_ROWS
            vT_ref[0, 0, r0:r0 + MLA_V_DIM, :] = vt[hh * MLA_V_DIM:(hh + 1) * MLA_V_DIM]
            vT_ref[0, 0, r0 + MLA_V_DIM:r0 + MLA_VT_ROWS, :] = ones


def _expand_kernel(c_ref, kr_ref, wuk_ref, wuv_ref, km_ref, vT_ref, *, valid_rows):
    _expand_tile(c_ref[0], kr_ref[0], wuk_ref, wuv_ref, km_ref.at[0], vT_ref, valid_rows=valid_rows)


def _expand(c, kr, wuk, wuv, *, tk, kv_len):
    B, T, _ = c.shape
    nkv = T // tk
    assert _n_features(tk) <= MLA_FEAT
    const = lambda b, j: (0, 0)
    tile = lambda b, j: (b, j, 0)
    return pl.pallas_call(
        functools.partial(_expand_kernel, valid_rows=kv_len - (nkv - 1) * tk),
        grid=(B, nkv),
        in_specs=[
            pl.BlockSpec((1, tk, MLA_KV_LORA), tile),
            pl.BlockSpec((1, tk, LANES), tile),
            pl.BlockSpec((MLA_KV_LORA, MLA_QK_COLS), const),
            pl.BlockSpec((MLA_KV_LORA, MLA_WIDTH), const),
        ],
        out_specs=[
            pl.BlockSpec((1, tk, MLA_QK_COLS), tile),
            pl.BlockSpec((1, 1, MLA_HEADS * MLA_VT_ROWS, tk), lambda b, j: (b, j, 0, 0)),
        ],
        out_shape=[
            jax.ShapeDtypeStruct((B, T, MLA_QK_COLS), BF16),
            jax.ShapeDtypeStruct((B, nkv, MLA_HEADS * MLA_VT_ROWS, tk), BF16),
        ],
        compiler_params=pltpu.CompilerParams(
            dimension_semantics=("arbitrary", "arbitrary"), vmem_limit_bytes=VMEM_LIMIT_BYTES),
        name="kv_expand",
    )(c, kr, wuk, wuv)


def _check_tiling(*, nq, tq, tk, nkv, q_off, kv_len):
    single = nq == 1 and nkv == 1
    causal = q_off == 0 and tq == tk and nq == nkv and kv_len == nkv * tk and (nq * (nq + 1) // 2) % 2 == 0
    assert single or causal, (nq, tq, tk, nkv, q_off, kv_len)


class _Chain(NamedTuple):
    load_k: Callable
    load_vT: Callable
    load_q: Callable
    s: object
    mx: object
    p: object
    a: object
    m: object
    acc: object


def _flash_flat(chains, nq, finalize):
    for c in chains:
        tk = c.s.shape[1]
        c.m[...] = jnp.full(c.m.shape, NEG_BIG, F32)
        c.acc[...] = jnp.zeros(c.acc.shape, F32)

    def advance(u):
        qi, j = u
        last = j == qi
        return jnp.where(last, qi + 1, qi), jnp.where(last, 0, j + 1)

    def scores(u, slot):
        qi, j = u
        for c in chains:
            s = jnp.dot(c.load_k(j), c.load_q(qi, j == qi), preferred_element_type=F32)
            c.s[slot] = s
            c.mx[slot] = jnp.max(s, axis=0, keepdims=True)

    def softmax(u, slot):
        qi, j = u
        for c in chains:
            m_prev = jnp.where(j == 0, NEG_BIG, c.m[...])
            m_new = jnp.maximum(m_prev, c.mx[slot])
            c.m[...] = m_new
            c.a[slot] = jnp.exp2(m_prev - m_new)
            for r in range(0, tk, BF16_ROWS):
                x = (c.s[slot, r:r + BF16_ROWS, :] - m_new).astype(BF16)
                c.p[slot, r:r + BF16_ROWS, :] = jnp.exp2(x)

    def values(u, slot):
        qi, j = u
        par = lax.rem(qi, 2)
        for c in chains:
            for vT, cs in c.load_vT(j):
                pv = jnp.dot(vT, c.p[slot, :, cs], preferred_element_type=F32)
                c.acc[par, :, cs] = c.a[slot, :, cs] * c.acc[par, :, cs] + pv

    def stages(ua, count):
        units = [ua]
        for _ in range(count + 1):
            units.append(advance(units[-1]))
        for i in range(count):
            cur, nxt = (i + 1) % 2, i % 2
            scores(units[i + 2], nxt)
            softmax(units[i + 1], cur)
            values(units[i], nxt)
        for i in range(0, count, 2):
            done_a = units[i][1] == units[i][0]
            done_b = units[i + 1][1] == units[i + 1][0]

            @pl.when(jnp.logical_or(done_a, done_b))
            def _():
                finalize(jnp.where(done_a, units[i][0], units[i + 1][0]))

        return units[count]

    zero = jnp.int32(0)
    u0 = (zero, zero)
    n_units = nq * (nq + 1) // 2
    if n_units == 1:
        scores(u0, 0)
        softmax(u0, 0)
        values(u0, 0)
        finalize(0)
        return

    scores(u0, 0)
    scores(advance(u0), 1)
    softmax(u0, 0)

    ua = lax.fori_loop(0, (n_units - 2) // LOOP_STAGES, lambda t, ua: stages(ua, LOOP_STAGES), u0)
    rest = (n_units - 2) % LOOP_STAGES
    if rest:
        ua = stages(ua, rest)
    ub = advance(ua)
    softmax(ub, 1)
    values(ua, 0)
    values(ub, 1)
    finalize(nq - 1)


def _ones_rows(tk):
    row = lax.broadcasted_iota(jnp.int32, (SUM_PAD, tk), 0)
    return jnp.where(row == 0, 1.0, 0.0).astype(BF16)


def _da_attn_kernel(lam_ref, hg_ref, qT_ref, k_ref, v_ref, o_ref,
                    vT_ref, qs_ref, feat_ref, gv_ref, s_ref, mx_ref, p_ref, a_ref, m_ref, acc_ref,
                    *, nq, tq, tk, nkv, q_off, kv_len, lam_init):
    valid_rows = kv_len - (nkv - 1) * tk
    feat_ref[...] = _key_features(tk, valid_rows, 0, BF16)
    q_rel = q_off if nkv == 1 else 0
    gv_ref[0] = _query_coeffs(LANES, 2 * tq, tk, tq, q_rel, diag=False)
    gv_ref[1] = _query_coeffs(LANES, 2 * tq, tk, tq, q_rel, diag=True)

    row = lax.broadcasted_iota(jnp.int32, (LANES, tq), 0)
    zero = jnp.zeros((LANES, tq), BF16)
    chains = []
    for hh in range(HEAD_PAIR):
        cols = slice(hh * LANES, (hh + 1) * LANES)
        for c in range(nkv):
            vT_ref[hh, c, :DA_V_DIM] = v_ref[0, c * tk:(c + 1) * tk, cols].astype(F32).T.astype(BF16)
            vT_ref[hh, c, DA_V_DIM:] = _ones_rows(tk)
        for qi in range(nq):
            qT = qT_ref[0, qi, cols, :]
            qs_ref[hh, qi, :, :tq] = jnp.where(row < DA_QK_DIM, qT, zero)
            qs_ref[hh, qi, :, tq:] = jnp.where(row >= DA_QK_DIM, qT, zero)
        chains.append(_Chain(
            load_k=lambda j, cols=cols: jnp.concatenate(
                [k_ref[0, pl.ds(pl.multiple_of(j * tk, tk), tk), cols], feat_ref[...]], axis=1),
            load_vT=lambda j, hh=hh: [(vT_ref[hh, j], slice(None))],
            load_q=lambda qi, diag, hh=hh: jnp.concatenate(
                [qs_ref[hh, qi], gv_ref[diag.astype(jnp.int32)]], axis=0),
            s=s_ref.at[hh], mx=mx_ref.at[hh], p=p_ref.at[hh], a=a_ref.at[hh], m=m_ref.at[hh],
            acc=acc_ref.at[hh]))

    lf = lam_ref[...]
    lam = (jnp.exp(jnp.sum(lf[0:1] * lf[1:2], axis=-1, keepdims=True))
           - jnp.exp(jnp.sum(lf[2:3] * lf[3:4], axis=-1, keepdims=True)) + lam_init)

    def finalize(qi):
        par = qi % 2
        row0 = qi * tq if isinstance(qi, int) else pl.multiple_of(qi * tq, tq)
        for hh in range(HEAD_PAIR):
            acc = acc_ref[hh, par]
            o = acc[:DA_V_DIM] * (1.0 / acc[DA_V_DIM:DA_V_DIM + 1])
            o = o[:, :tq] - lam * o[:, tq:]
            y = o * lax.rsqrt(jnp.mean(o * o, axis=0, keepdims=True) + RMS_EPS) * hg_ref[...] * (1.0 - lam_init)
            o_ref[0, pl.ds(row0, tq), hh * LANES:(hh + 1) * LANES] = y.T.astype(BF16)

    _flash_flat(chains, nq, finalize)


def _da_attention(lam, hg, qT, k, v, *, tk, q_off, kv_len, lam_init):
    B, nq, _, tq = qT.shape
    T = k.shape[1]
    nkv = T // tk
    _check_tiling(nq=nq, tq=tq, tk=tk, nkv=nkv, q_off=q_off, kv_len=kv_len)
    assert _n_features(tk) <= LANES
    pair = HEAD_PAIR * LANES
    head_kv = lambda b, h: (b, 0, h)
    return pl.pallas_call(
        functools.partial(_da_attn_kernel, nq=nq, tq=tq, tk=tk, nkv=nkv, q_off=q_off, kv_len=kv_len,
                          lam_init=lam_init),
        grid=(B, DA_HEADS // HEAD_PAIR),
        in_specs=[
            pl.BlockSpec(lam.shape, lambda b, h: (0, 0)),
            pl.BlockSpec((DA_V_DIM, 1), lambda b, h: (0, 0)),
            pl.BlockSpec((1, nq, pair, tq), lambda b, h: (b, 0, h, 0)),
            pl.BlockSpec((1, T, pair), head_kv),
            pl.BlockSpec((1, T, pair), head_kv),
        ],
        out_specs=pl.BlockSpec((1, nq * tq, pair), head_kv),
        out_shape=jax.ShapeDtypeStruct((B, nq * tq, DA_COLS), BF16),
        scratch_shapes=[
            pltpu.VMEM((HEAD_PAIR, nkv, DA_V_DIM + SUM_PAD, tk), BF16),
            pltpu.VMEM((HEAD_PAIR, nq, LANES, 2 * tq), BF16),
            pltpu.VMEM((tk, LANES), BF16),
            pltpu.VMEM((2, LANES, 2 * tq), BF16),
            pltpu.VMEM((HEAD_PAIR, 2, tk, 2 * tq), F32),
            pltpu.VMEM((HEAD_PAIR, 2, 1, 2 * tq), F32),
            pltpu.VMEM((HEAD_PAIR, 2, tk, 2 * tq), BF16),
            pltpu.VMEM((HEAD_PAIR, 2, 1, 2 * tq), F32),
            pltpu.VMEM((HEAD_PAIR, 1, 2 * tq), F32),
            pltpu.VMEM((HEAD_PAIR, 2, DA_V_DIM + SUM_PAD, 2 * tq), F32),
        ],
        compiler_params=pltpu.CompilerParams(
            dimension_semantics=("arbitrary", "arbitrary"), vmem_limit_bytes=VMEM_LIMIT_BYTES),
        name="da_attention",
    )(lam, hg, qT, k, v)


def _mla_attn_kernel(qT_ref, k_ref, vT_ref, o_ref, gv_ref, s_ref, mx_ref, p_ref, a_ref, m_ref, acc_ref,
                     *, nq, tq, tk, nkv, q_off):
    q_rel = q_off if nkv == 1 else 0
    gv_ref[0] = _query_coeffs(MLA_FEAT, tq, tk, tq, q_rel, diag=False)
    gv_ref[1] = _query_coeffs(MLA_FEAT, tq, tk, tq, q_rel, diag=True)
    zero = jnp.zeros((LANES, tq), BF16)
    halves = (slice(0, tq), slice(tq, 2 * tq))
    chains = []
    for cc in range(MLA_GROUP):
        h0 = 2 * cc

        def load_q(qi, diag, h0=h0):
            gv = gv_ref[diag.astype(jnp.int32)]
            qa, qb = (jnp.concatenate([qT_ref[0, qi, h * LANES:h * LANES + MLA_QK_DIM, :], gv], axis=0)
                      for h in (h0, h0 + 1))
            return jnp.concatenate([jnp.concatenate([qa, zero], axis=1),
                                    jnp.concatenate([zero, qb], axis=1)], axis=0)

        chains.append(_Chain(
            load_k=lambda j, h0=h0: k_ref[0, pl.ds(pl.multiple_of(j * tk, tk), tk),
                                          h0 * LANES:(h0 + 2) * LANES],
            load_vT=lambda j, h0=h0: [
                (vT_ref[0, j, (h0 + i) * MLA_VT_ROWS:(h0 + i + 1) * MLA_VT_ROWS, :], halves[i]) for i in range(2)],
            load_q=load_q,
            s=s_ref.at[cc], mx=mx_ref.at[cc], p=p_ref.at[cc], a=a_ref.at[cc], m=m_ref.at[cc],
            acc=acc_ref.at[cc]))

    def finalize(qi):
        par = qi % 2
        row0 = qi * tq if isinstance(qi, int) else pl.multiple_of(qi * tq, tq)
        for cc in range(MLA_GROUP):
            acc = acc_ref[cc, par]
            o = acc[:MLA_V_DIM] * (1.0 / acc[MLA_V_DIM:MLA_V_DIM + 1])
            o_ref[0, pl.ds(row0, tq), cc * LANES:(cc + 1) * LANES] = jnp.concatenate(
                [o[:, halves[0]], o[:, halves[1]]], axis=0).T.astype(BF16)

    _flash_flat(chains, nq, finalize)


def _mla_attention(qmT, km, vT, *, q_off, kv_len):
    B, nq, _, tq = qmT.shape
    nkv, tk = vT.shape[1], vT.shape[3]
    T = km.shape[1]
    _check_tiling(nq=nq, tq=tq, tk=tk, nkv=nkv, q_off=q_off, kv_len=kv_len)
    heads = 2 * MLA_GROUP
    ncols = 2 * tq
    return pl.pallas_call(
        functools.partial(_mla_attn_kernel, nq=nq, tq=tq, tk=tk, nkv=nkv, q_off=q_off),
        grid=(B, MLA_HEADS // heads),
        in_specs=[
            pl.BlockSpec((1, nq, heads * LANES, tq), lambda b, h: (b, 0, h, 0)),
            pl.BlockSpec((1, T, heads * LANES), lambda b, h: (b, 0, h)),
            pl.BlockSpec((1, nkv, heads * MLA_VT_ROWS, tk), lambda b, h: (b, 0, h, 0)),
        ],
        out_specs=pl.BlockSpec((1, nq * tq, heads * MLA_V_DIM), lambda b, h: (b, 0, h)),
        out_shape=jax.ShapeDtypeStruct((B, nq * tq, MLA_WIDTH), BF16),
        scratch_shapes=[
            pltpu.VMEM((2, MLA_FEAT, tq), BF16),
            pltpu.VMEM((MLA_GROUP, 2, tk, ncols), F32),
            pltpu.VMEM((MLA_GROUP, 2, 1, ncols), F32),
            pltpu.VMEM((MLA_GROUP, 2, tk, ncols), BF16),
            pltpu.VMEM((MLA_GROUP, 2, 1, ncols), F32),
            pltpu.VMEM((MLA_GROUP, 1, ncols), F32),
            pltpu.VMEM((MLA_GROUP, 2, MLA_VT_ROWS, ncols), F32),
        ],
        compiler_params=pltpu.CompilerParams(
            dimension_semantics=("arbitrary", "arbitrary"), vmem_limit_bytes=VMEM_LIMIT_BYTES),
        name="mla_attention",
    )(qmT, km, vT)


def _epilogue_kernel(x_ref, oa_ref, ob_ref, ng_ref, wzg_ref, gb_ref, wa_ref, wb_ref, wo_ref, fg_ref,
                     y_ref, *, final_norm):
    x = x_ref[...]
    d = x.shape[1]
    h = _rms(x, ng_ref[...]).astype(BF16)
    zg = jnp.dot(h, wzg_ref[...], preferred_element_type=F32)
    ga = (oa_ref[...].astype(F32) * jax.nn.silu(zg[:, :DA_COLS])).astype(BF16)
    gb = (ob_ref[...].astype(F32) * jax.nn.silu(zg[:, DA_COLS:DA_COLS + MLA_WIDTH])).astype(BF16)
    ya = jnp.dot(ga, wa_ref[...], preferred_element_type=F32)
    yb = jnp.dot(gb, wb_ref[...], preferred_element_type=F32)
    g = jax.nn.sigmoid(zg[:, DA_COLS + MLA_WIDTH:] + gb_ref[...])
    m = (g[:, :d] * ya + g[:, d:] * yb).astype(BF16)
    out = x + jnp.dot(m, wo_ref[...], preferred_element_type=F32)
    y_ref[...] = _rms(out, fg_ref[...]) if final_norm else out


def _epilogue(x, oa, ob, norm_g, wzg, gate_b, wa, wb, wo, final_g, *, tm, final_norm):
    n, D = x.shape
    const = lambda i: (0, 0)
    tok = lambda i: (i, 0)
    return pl.pallas_call(
        functools.partial(_epilogue_kernel, final_norm=final_norm),
        grid=(n // tm,),
        in_specs=[
            pl.BlockSpec((tm, D), tok),
            pl.BlockSpec((tm, DA_COLS), tok),
            pl.BlockSpec((tm, MLA_WIDTH), tok),
            pl.BlockSpec((1, D), const),
            pl.BlockSpec(wzg.shape, const),
            pl.BlockSpec((1, 2 * D), const),
            pl.BlockSpec(wa.shape, const),
            pl.BlockSpec(wb.shape, const),
            pl.BlockSpec(wo.shape, const),
            pl.BlockSpec((1, D), const),
        ],
        out_specs=pl.BlockSpec((tm, D), tok),
        out_shape=jax.ShapeDtypeStruct((n, D), F32),
        compiler_params=pltpu.CompilerParams(
            dimension_semantics=("arbitrary",), vmem_limit_bytes=VMEM_LIMIT_BYTES),
        name="epilogue",
    )(x, oa, ob, norm_g, wzg, gate_b, wa, wb, wo, final_g)


def _tiles(S):
    t = min(512, S)
    assert S % t == 0 and t % LANES == 0
    return t


def _prep_weights(w_in, mla_w_uq, mla_w_uk):
    D = w_in.shape[0]
    sizes = (DA_COLS, DA_COLS, DA_COLS, DA_COLS, MLA_Q_LORA, MLA_KV_LORA, MLA_ROPE, MLA_WIDTH, 2 * D)
    assert w_in.shape[1] == sum(sizes)
    w_q, w_k, w_v, w_za, w_cq, w_ckv, w_kr, w_zb, w_g = jnp.split(w_in, np.cumsum(sizes)[:-1].tolist(), axis=1)
    w_kr = jnp.pad(w_kr, ((0, 0), (MLA_NOPE, LANES - MLA_QK_DIM)))
    w_nat = jnp.concatenate([w_k, w_v, w_ckv, w_kr], axis=1).astype(BF16)
    w_t = jnp.concatenate([w_q, w_cq], axis=1).T.astype(BF16)
    w_zg = jnp.concatenate([w_za, w_zb, w_g], axis=1).astype(BF16)
    pad_heads = lambda w, dh: jnp.pad(
        w.reshape(w.shape[0], MLA_HEADS, dh), ((0, 0), (0, 0), (0, LANES - dh))
    ).reshape(w.shape[0], MLA_QK_COLS).astype(BF16)
    return w_nat, w_t, w_zg, pad_heads(mla_w_uq, MLA_QK_DIM).T, pad_heads(mla_w_uk, MLA_NOPE)


class _LayerWeights(NamedTuple):
    norm_g: jax.Array
    w_nat: jax.Array
    w_t: jax.Array
    w_zg: jax.Array
    gate_b: jax.Array
    da_lambda: jax.Array
    da_head_norm_g: jax.Array
    mla_q_norm_g: jax.Array
    mla_kv_norm_g: jax.Array
    wuqT: jax.Array
    wuk: jax.Array
    wuv: jax.Array
    wa: jax.Array
    wb: jax.Array
    wo: jax.Array
    final_norm_g: jax.Array


def _layer(x, past, lam_init, final_norm, w):
    B, S, D = x.shape
    past_len = 0 if past is None else past[0].shape[1]
    Sp = -(-S // LANES) * LANES
    xp = jnp.pad(x, ((0, 0), (0, Sp - S), (0, 0))) if Sp != S else x
    tm = _tiles(Sp)
    pos = past_len + jnp.arange(Sp, dtype=jnp.int32)
    tda = _rope_tables(pos, DA_ROT, _da_lane)
    tml = _rope_tables(pos, MLA_ROPE, _mla_lane)

    outs = _inproj(xp, w.norm_g, w.w_nat, w.w_t, w.mla_q_norm_g, w.mla_kv_norm_g, w.wuqT, tda, tml, w.wuk, w.wuv,
                   tm=tm, expand=past is None)
    qT, k_new, v_new, k_b, v_b, lat_new, kr_new, kr32, qmT = outs[:9]
    lat_new = lat_new.reshape(B, Sp, MLA_KV_LORA)

    kv_len = past_len + S
    if past is None:
        k_all, v_all = k_b.reshape(B, Sp, DA_COLS), v_b.reshape(B, Sp, DA_COLS)
        km, vmT = outs[9].reshape(B, Sp, MLA_QK_COLS), outs[10]
        tk = tm
    else:
        pk, pv, pc, pr = past
        tk = -(-kv_len // LANES) * LANES
        cat = lambda old, new: jnp.pad(jnp.concatenate([old, new[:, :S]], axis=1),
                                       ((0, 0), (0, tk - kv_len), (0, 0)))
        k_all = cat(pk.reshape(B, past_len, DA_COLS), k_new.reshape(B, Sp, DA_COLS)).astype(BF16)
        v_all = cat(pv.reshape(B, past_len, DA_COLS), v_new.reshape(B, Sp, DA_COLS)).astype(BF16)
        c_all = cat(pc, lat_new)
        kr_all = cat(jnp.pad(pr, ((0, 0), (0, 0), (MLA_NOPE, LANES - MLA_QK_DIM))), kr_new.reshape(B, Sp, LANES))
        km, vmT = _expand(c_all, kr_all, w.wuk, w.wuv, tk=tk, kv_len=kv_len)

    o_a = _da_attention(w.da_lambda, w.da_head_norm_g, qT, k_all, v_all,
                        tk=tk, q_off=past_len, kv_len=kv_len, lam_init=lam_init)
    o_b = _mla_attention(qmT, km, vmT, q_off=past_len, kv_len=kv_len)

    n = B * Sp
    y = _epilogue(xp.reshape(n, D), o_a.reshape(n, DA_COLS), o_b.reshape(n, MLA_WIDTH), w.norm_g, w.w_zg,
                  w.gate_b, w.wa, w.wb, w.wo, w.final_norm_g, tm=_tiles(n), final_norm=final_norm)
    y = y.reshape(B, Sp, D)[:, :S]
    new = (k_new[:, :S], v_new[:, :S], lat_new[:, :S], kr32.reshape(B, Sp, MLA_ROPE)[:, :S])
    return y, new


def kernel(x_prompt, x_sample, cache_da_k, cache_da_v, cache_mla_latent, cache_mla_krope, norm_g, w_in, gate_b, da_lambda, da_head_norm_g, mla_q_norm_g, mla_kv_norm_g, mla_w_uq, mla_w_uk, mla_w_uv, w_branch_a, w_branch_b, w_out, final_norm_g):
    depth = w_in.shape[0]
    at = lambda a, l: a.reshape(a.shape[1:]) if depth == 1 else a[l]
    row = lambda v: v.reshape(1, -1)
    hp, hs = x_prompt, x_sample
    rows_p, rows_s = [], []
    for l in range(depth):
        lam_init = 0.8 - 0.6 * math.exp(-0.3 * l)
        last = l == depth - 1
        w_nat, w_t, w_zg, wuqT, wuk = _prep_weights(at(w_in, l), at(mla_w_uq, l), at(mla_w_uk, l))
        w = _LayerWeights(
            norm_g=row(at(norm_g, l)), w_nat=w_nat, w_t=w_t, w_zg=w_zg, gate_b=row(at(gate_b, l)),
            da_lambda=at(da_lambda, l), da_head_norm_g=at(da_head_norm_g, l).reshape(-1, 1),
            mla_q_norm_g=at(mla_q_norm_g, l).reshape(-1, 1), mla_kv_norm_g=row(at(mla_kv_norm_g, l)),
            wuqT=wuqT, wuk=wuk, wuv=at(mla_w_uv, l).astype(BF16), wa=at(w_branch_a, l).astype(BF16),
            wb=at(w_branch_b, l).astype(BF16), wo=at(w_out, l).astype(BF16), final_norm_g=row(final_norm_g))
        hp, new_p = _layer(hp, None, lam_init, last, w)
        past = (at(cache_da_k, l), at(cache_da_v, l), at(cache_mla_latent, l), at(cache_mla_krope, l))
        hs, new_s = _layer(hs, past, lam_init, last, w)
        rows_p.append(new_p)
        rows_s.append(new_s)
    stack = lambda rows, i: jnp.stack([r[i] for r in rows], 0)
    return (hp, hs, stack(rows_p, 0), stack(rows_p, 1), stack(rows_p, 2), stack(rows_p, 3),
            stack(rows_s, 0), stack(rows_s, 1), stack(rows_s, 2), stack(rows_s, 3))
```

```python
import functools
import math
from typing import Callable, NamedTuple

import jax
import jax.numpy as jnp
import numpy as np
from jax import lax
from jax.experimental import pallas as pl
from jax.experimental.pallas import tpu as pltpu

CHUNK = 64
ROPE_THETA = 500000.0
RMS_EPS = 1e-6
DA_HEADS = 4
DA_QK_DIM = 64
DA_V_DIM = 2 * DA_QK_DIM
DA_ROT = DA_QK_DIM // 4
DA_COLS = DA_HEADS * DA_V_DIM
MLA_HEADS = 8
MLA_Q_LORA = 384
MLA_KV_LORA = 256
MLA_NOPE = 64
MLA_ROPE = 32
MLA_V_DIM = 64
MLA_WIDTH = MLA_HEADS * MLA_V_DIM
MLA_QK_DIM = MLA_NOPE + MLA_ROPE

LANES = 128
SUM_ROWS = 8
BF16_ROWS = 16
SUM_PAD = BF16_ROWS
MLA_VT_ROWS = 64 + SUM_PAD
MLA_QK_COLS = MLA_HEADS * LANES
MLA_FEAT = LANES - MLA_QK_DIM
LOG2E = math.log2(math.e)
NEG_BIG = -1e30
VMEM_LIMIT_BYTES = 56 * 1024 * 1024
HEAD_PAIR = 2
MLA_GROUP = 2
LOOP_STAGES = 2

F32 = jnp.float32
BF16 = jnp.bfloat16


def _rms(x, g):
    return x * lax.rsqrt(jnp.mean(x * x, axis=-1, keepdims=True) + RMS_EPS) * g


def _rope128(x, tab_ref, shift):
    return (x * tab_ref[0] + pltpu.roll(x, LANES - shift, 1) * tab_ref[1]
            + pltpu.roll(x, shift, 1) * tab_ref[2])


def _rope_tables(pos, rot, lane_of):
    half = rot // 2
    inv = jnp.float32(ROPE_THETA) ** (-jnp.arange(half, dtype=F32) * 2.0 / rot)
    ang = pos.astype(F32)[:, None] * inv
    cos, sin = jnp.cos(ang), jnp.sin(ang)
    idx = np.array([lane_of(j) for j in range(LANES)])
    lo = (idx >= 0) & (idx < half)
    hi = idx >= half
    src = np.where(idx >= 0, idx % half, 0)
    c = jnp.where(jnp.asarray(lo | hi), cos[:, src], 1.0)
    sa = jnp.where(jnp.asarray(lo), -sin[:, src], 0.0)
    sb = jnp.where(jnp.asarray(hi), sin[:, src], 0.0)
    return jnp.stack([c, sa, sb]).astype(F32)


def _da_lane(j):
    c = j % DA_QK_DIM
    return c if c < DA_ROT else -1


def _mla_lane(j):
    c = j - MLA_NOPE
    return c if 0 <= c < MLA_ROPE else -1


def _rope_rows(x, tab_ref, shift):
    up = jnp.concatenate([x[shift:], x[:shift]], axis=0)
    dn = jnp.concatenate([x[-shift:], x[:-shift]], axis=0)
    return x * tab_ref[0] + up * tab_ref[1] + dn * tab_ref[2]


def _inproj_kernel(x_ref, ng_ref, wn_ref, wt_ref, qg_ref, kvg_ref, wuqT_ref, tda_ref, tml_ref, tdaT_ref,
                   tmlT_ref, *refs, q_scale, qm_scale, expand):
    if expand:
        wuk_ref, wuv_ref = refs[:2]
        refs = refs[2:]
    qT_ref, k4_ref, v4_ref, kb_ref, vb_ref, lat_ref, kr_ref, kr32_ref, qmT_ref = refs[:9]
    o_v = DA_COLS
    o_ckv = 2 * DA_COLS
    o_kr = o_ckv + MLA_KV_LORA
    xn = _rms(x_ref[...], ng_ref[...])
    h = xn.astype(BF16)

    proj = jnp.dot(h, wn_ref[...], preferred_element_type=F32)
    for i in range(DA_HEADS):
        lo, hi = i * LANES, (i + 1) * LANES
        k = _rope128(proj[:, lo:hi], tda_ref, DA_ROT // 2)
        v = proj[:, o_v + lo:o_v + hi]
        k4_ref[0, :, i, :] = k
        v4_ref[0, :, i, :] = v
        kb_ref[:, lo:hi] = k.astype(BF16)
        vb_ref[:, lo:hi] = v.astype(BF16)
    lat = _rms(proj[:, o_ckv:o_ckv + MLA_KV_LORA], kvg_ref[...])
    lat_ref[...] = lat
    kr = _rope128(proj[:, o_kr:o_kr + LANES], tml_ref, MLA_ROPE // 2)
    kr_ref[...] = kr
    kr32_ref[...] = kr[:, MLA_NOPE:MLA_QK_DIM]
    if expand:
        _expand_tile(lat, kr, wuk_ref, wuv_ref, refs[9], refs[10], valid_rows=x_ref.shape[0])

    d = xn.shape[1]
    hT = jnp.concatenate([xn[:, c:c + LANES].T.astype(BF16) for c in range(0, d, LANES)], axis=0)
    projT = jnp.dot(wt_ref[...], hT, preferred_element_type=F32)
    for i in range(DA_HEADS):
        lo, hi = i * LANES, (i + 1) * LANES
        qT_ref[0, 0, lo:hi, :] = (_rope_rows(projT[lo:hi], tdaT_ref, DA_ROT // 2) * q_scale).astype(BF16)
    cqT = projT[DA_COLS:]
    qnT = (cqT * lax.rsqrt(jnp.mean(cqT * cqT, axis=0, keepdims=True) + RMS_EPS) * qg_ref[...]).astype(BF16)
    qmT = jnp.dot(wuqT_ref[...], qnT, preferred_element_type=F32)
    for i in range(MLA_HEADS):
        lo, hi = i * LANES, (i + 1) * LANES
        qmT_ref[0, 0, lo:hi, :] = (_rope_rows(qmT[lo:hi], tmlT_ref, MLA_ROPE // 2) * qm_scale).astype(BF16)


def _inproj(x, norm_g, w_nat, w_t, qg_col, kvg, wuqT, tda, tml, wuk, wuv, *, tm, expand):
    B, S, D = x.shape
    n = B * S
    nst = S // tm
    const = lambda i: (0, 0)
    tok = lambda i: (i, 0)
    feat_t = lambda i: (i // nst, i % nst, 0, 0)
    tab = lambda i: (0, i % nst, 0)
    tabT = lambda i: (0, 0, i % nst)
    in_specs = [
        pl.BlockSpec((tm, D), tok),
        pl.BlockSpec((1, D), const),
        pl.BlockSpec(w_nat.shape, const),
        pl.BlockSpec(w_t.shape, const),
        pl.BlockSpec((MLA_Q_LORA, 1), const),
        pl.BlockSpec((1, MLA_KV_LORA), const),
        pl.BlockSpec((MLA_QK_COLS, MLA_Q_LORA), const),
        pl.BlockSpec((3, tm, LANES), tab),
        pl.BlockSpec((3, tm, LANES), tab),
        pl.BlockSpec((3, LANES, tm), tabT),
        pl.BlockSpec((3, LANES, tm), tabT),
    ]
    out_specs = [
        pl.BlockSpec((1, 1, DA_COLS, tm), feat_t),
        pl.BlockSpec((1, tm, DA_HEADS, DA_V_DIM), feat_t),
        pl.BlockSpec((1, tm, DA_HEADS, DA_V_DIM), feat_t),
        pl.BlockSpec((tm, DA_COLS), tok),
        pl.BlockSpec((tm, DA_COLS), tok),
        pl.BlockSpec((tm, MLA_KV_LORA), tok),
        pl.BlockSpec((tm, LANES), tok),
        pl.BlockSpec((tm, MLA_ROPE), tok),
        pl.BlockSpec((1, 1, MLA_QK_COLS, tm), feat_t),
    ]
    out_shape = [
        jax.ShapeDtypeStruct((B, nst, DA_COLS, tm), BF16),
        jax.ShapeDtypeStruct((B, S, DA_HEADS, DA_V_DIM), F32),
        jax.ShapeDtypeStruct((B, S, DA_HEADS, DA_V_DIM), F32),
        jax.ShapeDtypeStruct((n, DA_COLS), BF16),
        jax.ShapeDtypeStruct((n, DA_COLS), BF16),
        jax.ShapeDtypeStruct((n, MLA_KV_LORA), F32),
        jax.ShapeDtypeStruct((n, LANES), F32),
        jax.ShapeDtypeStruct((n, MLA_ROPE), F32),
        jax.ShapeDtypeStruct((B, nst, MLA_QK_COLS, tm), BF16),
    ]
    args = [x.reshape(n, D), norm_g, w_nat, w_t, qg_col, kvg, wuqT, tda, tml,
            tda.transpose(0, 2, 1), tml.transpose(0, 2, 1)]
    if expand:
        assert _n_features(tm) <= MLA_FEAT
        in_specs += [pl.BlockSpec(wuk.shape, const), pl.BlockSpec(wuv.shape, const)]
        out_specs += [pl.BlockSpec((tm, MLA_QK_COLS), tok),
                      pl.BlockSpec((1, 1, MLA_HEADS * MLA_VT_ROWS, tm), feat_t)]
        out_shape += [jax.ShapeDtypeStruct((n, MLA_QK_COLS), BF16),
                      jax.ShapeDtypeStruct((B, nst, MLA_HEADS * MLA_VT_ROWS, tm), BF16)]
        args += [wuk, wuv]
    return pl.pallas_call(
        functools.partial(_inproj_kernel, q_scale=DA_QK_DIM ** -0.5 * LOG2E,
                          qm_scale=MLA_QK_DIM ** -0.5 * LOG2E, expand=expand),
        grid=(n // tm,),
        in_specs=in_specs,
        out_specs=out_specs,
        out_shape=out_shape,
        compiler_params=pltpu.CompilerParams(
            dimension_semantics=("arbitrary",), vmem_limit_bytes=VMEM_LIMIT_BYTES),
        name="inproj",
    )(*args)


def _n_features(tk):
    return tk // CHUNK + 1


def _key_features(tk, valid_rows, lane0, dtype):
    nf = _n_features(tk)
    row = lax.broadcasted_iota(jnp.int32, (tk, LANES), 0)
    f = lax.broadcasted_iota(jnp.int32, (tk, LANES), 1) - lane0
    chunk_hit = (f >= 0) & (f < nf - 1) & (lax.shift_right_logical(row, 6) == f)
    pad_hit = (f == nf - 1) & (row >= valid_rows)
    return jnp.where(chunk_hit | pad_hit, 1.0, 0.0).astype(dtype)


def _query_coeffs(nrows, ncols, tk, tq, q_rel, diag):
    nf = _n_features(tk)
    r = lax.broadcasted_iota(jnp.int32, (nrows, ncols), 0)
    col = lax.broadcasted_iota(jnp.int32, (nrows, ncols), 1)
    q_chunk = lax.shift_right_logical(q_rel + jnp.where(col >= tq, col - tq, col), 6)
    hidden = (r == nf - 1)
    if diag:
        hidden = hidden | ((r < nf - 1) & (r > q_chunk))
    return jnp.where(hidden, NEG_BIG, 0.0).astype(BF16)


def _expand_tile(lat, kr, wuk_ref, wuv_ref, km_ref, vT_ref, *, valid_rows):
    tk = lat.shape[0]
    c = lat.astype(BF16)
    kn = jnp.dot(c, wuk_ref[...], preferred_element_type=F32)
    tail = kr + _key_features(tk, valid_rows, MLA_QK_DIM, F32)
    for i in range(MLA_HEADS):
        lo, hi = i * LANES, (i + 1) * LANES
        km_ref[:, lo:hi] = (kn[:, lo:hi] + tail).astype(BF16)
    vm = jnp.dot(c, wuv_ref[...], preferred_element_type=F32)
    ones = _ones_rows(tk)
    for i in range(MLA_WIDTH // LANES):
        vt = vm[:, i * LANES:(i + 1) * LANES].T.astype(BF16)
        for hh in range(LANES // MLA_V_DIM):
            r0 = (i * (LANES // MLA_V_DIM) + hh) * MLA_VT_ROWS
            vT_ref[0, 0, r0:r0 + MLA_V_DIM, :] = vt[hh * MLA_V_DIM:(hh + 1) * MLA_V_DIM]
            vT_ref[0, 0, r0 + MLA_V_DIM:r0 + MLA_VT_ROWS, :] = ones


def _expand_kernel(c_ref, kr_ref, wuk_ref, wuv_ref, km_ref, vT_ref, *, valid_rows):
    _expand_tile(c_ref[0], kr_ref[0], wuk_ref, wuv_ref, km_ref.at[0], vT_ref, valid_rows=valid_rows)


def _expand(c, kr, wuk, wuv, *, tk, kv_len):
    B, T, _ = c.shape
    nkv = T // tk
    assert _n_features(tk) <= MLA_FEAT
    const = lambda b, j: (0, 0)
    tile = lambda b, j: (b, j, 0)
    return pl.pallas_call(
        functools.partial(_expand_kernel, valid_rows=kv_len - (nkv - 1) * tk),
        grid=(B, nkv),
        in_specs=[
            pl.BlockSpec((1, tk, MLA_KV_LORA), tile),
            pl.BlockSpec((1, tk, LANES), tile),
            pl.BlockSpec((MLA_KV_LORA, MLA_QK_COLS), const),
            pl.BlockSpec((MLA_KV_LORA, MLA_WIDTH), const),
        ],
        out_specs=[
            pl.BlockSpec((1, tk, MLA_QK_COLS), tile),
            pl.BlockSpec((1, 1, MLA_HEADS * MLA_VT_ROWS, tk), lambda b, j: (b, j, 0, 0)),
        ],
        out_shape=[
            jax.ShapeDtypeStruct((B, T, MLA_QK_COLS), BF16),
            jax.ShapeDtypeStruct((B, nkv, MLA_HEADS * MLA_VT_ROWS, tk), BF16),
        ],
        compiler_params=pltpu.CompilerParams(
            dimension_semantics=("arbitrary", "arbitrary"), vmem_limit_bytes=VMEM_LIMIT_BYTES),
        name="kv_expand",
    )(c, kr, wuk, wuv)


def _check_tiling(*, nq, tq, tk, nkv, q_off, kv_len):
    single = nq == 1 and nkv == 1
    causal = q_off == 0 and tq == tk and nq == nkv and kv_len == nkv * tk and (nq * (nq + 1) // 2) % 2 == 0
    assert single or causal, (nq, tq, tk, nkv, q_off, kv_len)


class _Chain(NamedTuple):
    load_k: Callable
    load_vT: Callable
    load_q: Callable
    s: object
    mx: object
    p: object
    a: object
    m: object
    acc: object


def _flash_flat(chains, nq, finalize):
    for c in chains:
        tk = c.s.shape[1]
        c.m[...] = jnp.full(c.m.shape, NEG_BIG, F32)
        c.acc[...] = jnp.zeros(c.acc.shape, F32)

    def advance(u):
        qi, j = u
        last = j == qi
        return jnp.where(last, qi + 1, qi), jnp.where(last, 0, j + 1)

    def scores(u, slot):
        qi, j = u
        for c in chains:
            s = jnp.dot(c.load_k(j), c.load_q(qi, j == qi), preferred_element_type=F32)
            c.s[slot] = s
            c.mx[slot] = jnp.max(s, axis=0, keepdims=True)

    def softmax(u, slot):
        qi, j = u
        for c in chains:
            m_prev = jnp.where(j == 0, NEG_BIG, c.m[...])
            m_new = jnp.maximum(m_prev, c.mx[slot])
            c.m[...] = m_new
            c.a[slot] = jnp.exp2(m_prev - m_new)
            for r in range(0, tk, BF16_ROWS):
                x = (c.s[slot, r:r + BF16_ROWS, :] - m_new).astype(BF16)
                c.p[slot, r:r + BF16_ROWS, :] = jnp.exp2(x)

    def values(u, slot):
        qi, j = u
        par = lax.rem(qi, 2)
        for c in chains:
            for vT, cs in c.load_vT(j):
                pv = jnp.dot(vT, c.p[slot, :, cs], preferred_element_type=F32)
                c.acc[par, :, cs] = c.a[slot, :, cs] * c.acc[par, :, cs] + pv

    def stages(ua, count):
        units = [ua]
        for _ in range(count + 1):
            units.append(advance(units[-1]))
        for i in range(count):
            cur, nxt = (i + 1) % 2, i % 2
            scores(units[i + 2], nxt)
            softmax(units[i + 1], cur)
            values(units[i], nxt)
        for i in range(0, count, 2):
            done_a = units[i][1] == units[i][0]
            done_b = units[i + 1][1] == units[i + 1][0]

            @pl.when(jnp.logical_or(done_a, done_b))
            def _():
                finalize(jnp.where(done_a, units[i][0], units[i + 1][0]))

        return units[count]

    zero = jnp.int32(0)
    u0 = (zero, zero)
    n_units = nq * (nq + 1) // 2
    if n_units == 1:
        scores(u0, 0)
        softmax(u0, 0)
        values(u0, 0)
        finalize(0)
        return

    scores(u0, 0)
    scores(advance(u0), 1)
    softmax(u0, 0)

    ua = lax.fori_loop(0, (n_units - 2) // LOOP_STAGES, lambda t, ua: stages(ua, LOOP_STAGES), u0)
    rest = (n_units - 2) % LOOP_STAGES
    if rest:
        ua = stages(ua, rest)
    ub = advance(ua)
    softmax(ub, 1)
    values(ua, 0)
    values(ub, 1)
    finalize(nq - 1)


def _ones_rows(tk):
    row = lax.broadcasted_iota(jnp.int32, (SUM_PAD, tk), 0)
    return jnp.where(row == 0, 1.0, 0.0).astype(BF16)


def _da_attn_kernel(lam_ref, hg_ref, qT_ref, k_ref, v_ref, o_ref,
                    vT_ref, qs_ref, feat_ref, gv_ref, s_ref, mx_ref, p_ref, a_ref, m_ref, acc_ref,
                    *, nq, tq, tk, nkv, q_off, kv_len, lam_init):
    valid_rows = kv_len - (nkv - 1) * tk
    feat_ref[...] = _key_features(tk, valid_rows, 0, BF16)
    q_rel = q_off if nkv == 1 else 0
    gv_ref[0] = _query_coeffs(LANES, 2 * tq, tk, tq, q_rel, diag=False)
    gv_ref[1] = _query_coeffs(LANES, 2 * tq, tk, tq, q_rel, diag=True)

    row = lax.broadcasted_iota(jnp.int32, (LANES, tq), 0)
    zero = jnp.zeros((LANES, tq), BF16)
    chains = []
    for hh in range(HEAD_PAIR):
        cols = slice(hh * LANES, (hh + 1) * LANES)
        for c in range(nkv):
            vT_ref[hh, c, :DA_V_DIM] = v_ref[0, c * tk:(c + 1) * tk, cols].astype(F32).T.astype(BF16)
            vT_ref[hh, c, DA_V_DIM:] = _ones_rows(tk)
        for qi in range(nq):
            qT = qT_ref[0, qi, cols, :]
            qs_ref[hh, qi, :, :tq] = jnp.where(row < DA_QK_DIM, qT, zero)
            qs_ref[hh, qi, :, tq:] = jnp.where(row >= DA_QK_DIM, qT, zero)
        chains.append(_Chain(
            load_k=lambda j, cols=cols: jnp.concatenate(
                [k_ref[0, pl.ds(pl.multiple_of(j * tk, tk), tk), cols], feat_ref[...]], axis=1),
            load_vT=lambda j, hh=hh: [(vT_ref[hh, j], slice(None))],
            load_q=lambda qi, diag, hh=hh: jnp.concatenate(
                [qs_ref[hh, qi], gv_ref[diag.astype(jnp.int32)]], axis=0),
            s=s_ref.at[hh], mx=mx_ref.at[hh], p=p_ref.at[hh], a=a_ref.at[hh], m=m_ref.at[hh],
            acc=acc_ref.at[hh]))

    lf = lam_ref[...]
    lam = (jnp.exp(jnp.sum(lf[0:1] * lf[1:2], axis=-1, keepdims=True))
           - jnp.exp(jnp.sum(lf[2:3] * lf[3:4], axis=-1, keepdims=True)) + lam_init)

    def finalize(qi):
        par = qi % 2
        row0 = qi * tq if isinstance(qi, int) else pl.multiple_of(qi * tq, tq)
        for hh in range(HEAD_PAIR):
            acc = acc_ref[hh, par]
            o = acc[:DA_V_DIM] * (1.0 / acc[DA_V_DIM:DA_V_DIM + 1])
            o = o[:, :tq] - lam * o[:, tq:]
            y = o * lax.rsqrt(jnp.mean(o * o, axis=0, keepdims=True) + RMS_EPS) * hg_ref[...] * (1.0 - lam_init)
            o_ref[0, pl.ds(row0, tq), hh * LANES:(hh + 1) * LANES] = y.T.astype(BF16)

    _flash_flat(chains, nq, finalize)


def _da_attention(lam, hg, qT, k, v, *, tk, q_off, kv_len, lam_init):
    B, nq, _, tq = qT.shape
    T = k.shape[1]
    nkv = T // tk
    _check_tiling(nq=nq, tq=tq, tk=tk, nkv=nkv, q_off=q_off, kv_len=kv_len)
    assert _n_features(tk) <= LANES
    pair = HEAD_PAIR * LANES
    head_kv = lambda b, h: (b, 0, h)
    return pl.pallas_call(
        functools.partial(_da_attn_kernel, nq=nq, tq=tq, tk=tk, nkv=nkv, q_off=q_off, kv_len=kv_len,
                          lam_init=lam_init),
        grid=(B, DA_HEADS // HEAD_PAIR),
        in_specs=[
            pl.BlockSpec(lam.shape, lambda b, h: (0, 0)),
            pl.BlockSpec((DA_V_DIM, 1), lambda b, h: (0, 0)),
            pl.BlockSpec((1, nq, pair, tq), lambda b, h: (b, 0, h, 0)),
            pl.BlockSpec((1, T, pair), head_kv),
            pl.BlockSpec((1, T, pair), head_kv),
        ],
        out_specs=pl.BlockSpec((1, nq * tq, pair), head_kv),
        out_shape=jax.ShapeDtypeStruct((B, nq * tq, DA_COLS), BF16),
        scratch_shapes=[
            pltpu.VMEM((HEAD_PAIR, nkv, DA_V_DIM + SUM_PAD, tk), BF16),
            pltpu.VMEM((HEAD_PAIR, nq, LANES, 2 * tq), BF16),
            pltpu.VMEM((tk, LANES), BF16),
            pltpu.VMEM((2, LANES, 2 * tq), BF16),
            pltpu.VMEM((HEAD_PAIR, 2, tk, 2 * tq), F32),
            pltpu.VMEM((HEAD_PAIR, 2, 1, 2 * tq), F32),
            pltpu.VMEM((HEAD_PAIR, 2, tk, 2 * tq), BF16),
            pltpu.VMEM((HEAD_PAIR, 2, 1, 2 * tq), F32),
            pltpu.VMEM((HEAD_PAIR, 1, 2 * tq), F32),
            pltpu.VMEM((HEAD_PAIR, 2, DA_V_DIM + SUM_PAD, 2 * tq), F32),
        ],
        compiler_params=pltpu.CompilerParams(
            dimension_semantics=("arbitrary", "arbitrary"), vmem_limit_bytes=VMEM_LIMIT_BYTES),
        name="da_attention",
    )(lam, hg, qT, k, v)


def _mla_attn_kernel(qT_ref, k_ref, vT_ref, o_ref, gv_ref, s_ref, mx_ref, p_ref, a_ref, m_ref, acc_ref,
                     *, nq, tq, tk, nkv, q_off):
    q_rel = q_off if nkv == 1 else 0
    gv_ref[0] = _query_coeffs(MLA_FEAT, tq, tk, tq, q_rel, diag=False)
    gv_ref[1] = _query_coeffs(MLA_FEAT, tq, tk, tq, q_rel, diag=True)
    zero = jnp.zeros((LANES, tq), BF16)
    halves = (slice(0, tq), slice(tq, 2 * tq))
    chains = []
    for cc in range(MLA_GROUP):
        h0 = 2 * cc

        def load_q(qi, diag, h0=h0):
            gv = gv_ref[diag.astype(jnp.int32)]
            qa, qb = (jnp.concatenate([qT_ref[0, qi, h * LANES:h * LANES + MLA_QK_DIM, :], gv], axis=0)
                      for h in (h0, h0 + 1))
            return jnp.concatenate([jnp.concatenate([qa, zero], axis=1),
                                    jnp.concatenate([zero, qb], axis=1)], axis=0)

        chains.append(_Chain(
            load_k=lambda j, h0=h0: k_ref[0, pl.ds(pl.multiple_of(j * tk, tk), tk),
                                          h0 * LANES:(h0 + 2) * LANES],
            load_vT=lambda j, h0=h0: [
                (vT_ref[0, j, (h0 + i) * MLA_VT_ROWS:(h0 + i + 1) * MLA_VT_ROWS, :], halves[i]) for i in range(2)],
            load_q=load_q,
            s=s_ref.at[cc], mx=mx_ref.at[cc], p=p_ref.at[cc], a=a_ref.at[cc], m=m_ref.at[cc],
            acc=acc_ref.at[cc]))

    def finalize(qi):
        par = qi % 2
        row0 = qi * tq if isinstance(qi, int) else pl.multiple_of(qi * tq, tq)
        for cc in range(MLA_GROUP):
            acc = acc_ref[cc, par]
            o = acc[:MLA_V_DIM] * (1.0 / acc[MLA_V_DIM:MLA_V_DIM + 1])
            o_ref[0, pl.ds(row0, tq), cc * LANES:(cc + 1) * LANES] = jnp.concatenate(
                [o[:, halves[0]], o[:, halves[1]]], axis=0).T.astype(BF16)

    _flash_flat(chains, nq, finalize)


def _mla_attention(qmT, km, vT, *, q_off, kv_len):
    B, nq, _, tq = qmT.shape
    nkv, tk = vT.shape[1], vT.shape[3]
    T = km.shape[1]
    _check_tiling(nq=nq, tq=tq, tk=tk, nkv=nkv, q_off=q_off, kv_len=kv_len)
    heads = 2 * MLA_GROUP
    ncols = 2 * tq
    return pl.pallas_call(
        functools.partial(_mla_attn_kernel, nq=nq, tq=tq, tk=tk, nkv=nkv, q_off=q_off),
        grid=(B, MLA_HEADS // heads),
        in_specs=[
            pl.BlockSpec((1, nq, heads * LANES, tq), lambda b, h: (b, 0, h, 0)),
            pl.BlockSpec((1, T, heads * LANES), lambda b, h: (b, 0, h)),
            pl.BlockSpec((1, nkv, heads * MLA_VT_ROWS, tk), lambda b, h: (b, 0, h, 0)),
        ],
        out_specs=pl.BlockSpec((1, nq * tq, heads * MLA_V_DIM), lambda b, h: (b, 0, h)),
        out_shape=jax.ShapeDtypeStruct((B, nq * tq, MLA_WIDTH), BF16),
        scratch_shapes=[
            pltpu.VMEM((2, MLA_FEAT, tq), BF16),
            pltpu.VMEM((MLA_GROUP, 2, tk, ncols), F32),
            pltpu.VMEM((MLA_GROUP, 2, 1, ncols), F32),
            pltpu.VMEM((MLA_GROUP, 2, tk, ncols), BF16),
            pltpu.VMEM((MLA_GROUP, 2, 1, ncols), F32),
            pltpu.VMEM((MLA_GROUP, 1, ncols), F32),
            pltpu.VMEM((MLA_GROUP, 2, MLA_VT_ROWS, ncols), F32),
        ],
        compiler_params=pltpu.CompilerParams(
            dimension_semantics=("arbitrary", "arbitrary"), vmem_limit_bytes=VMEM_LIMIT_BYTES),
        name="mla_attention",
    )(qmT, km, vT)


def _epilogue_kernel(x_ref, oa_ref, ob_ref, ng_ref, wzg_ref, gb_ref, wa_ref, wb_ref, wo_ref, fg_ref,
                     y_ref, *, final_norm):
    x = x_ref[...]
    d = x.shape[1]
    h = _rms(x, ng_ref[...]).astype(BF16)
    zg = jnp.dot(h, wzg_ref[...], preferred_element_type=F32)
    ga = (oa_ref[...].astype(F32) * jax.nn.silu(zg[:, :DA_COLS])).astype(BF16)
    gb = (ob_ref[...].astype(F32) * jax.nn.silu(zg[:, DA_COLS:DA_COLS + MLA_WIDTH])).astype(BF16)
    ya = jnp.dot(ga, wa_ref[...], preferred_element_type=F32)
    yb = jnp.dot(gb, wb_ref[...], preferred_element_type=F32)
    g = jax.nn.sigmoid(zg[:, DA_COLS + MLA_WIDTH:] + gb_ref[...])
    m = (g[:, :d] * ya + g[:, d:] * yb).astype(BF16)
    out = x + jnp.dot(m, wo_ref[...], preferred_element_type=F32)
    y_ref[...] = _rms(out, fg_ref[...]) if final_norm else out


def _epilogue(x, oa, ob, norm_g, wzg, gate_b, wa, wb, wo, final_g, *, tm, final_norm):
    n, D = x.shape
    const = lambda i: (0, 0)
    tok = lambda i: (i, 0)
    return pl.pallas_call(
        functools.partial(_epilogue_kernel, final_norm=final_norm),
        grid=(n // tm,),
        in_specs=[
            pl.BlockSpec((tm, D), tok),
            pl.BlockSpec((tm, DA_COLS), tok),
            pl.BlockSpec((tm, MLA_WIDTH), tok),
            pl.BlockSpec((1, D), const),
            pl.BlockSpec(wzg.shape, const),
            pl.BlockSpec((1, 2 * D), const),
            pl.BlockSpec(wa.shape, const),
            pl.BlockSpec(wb.shape, const),
            pl.BlockSpec(wo.shape, const),
            pl.BlockSpec((1, D), const),
        ],
        out_specs=pl.BlockSpec((tm, D), tok),
        out_shape=jax.ShapeDtypeStruct((n, D), F32),
        compiler_params=pltpu.CompilerParams(
            dimension_semantics=("arbitrary",), vmem_limit_bytes=VMEM_LIMIT_BYTES),
        name="epilogue",
    )(x, oa, ob, norm_g, wzg, gate_b, wa, wb, wo, final_g)


def _tiles(S):
    t = min(512, S)
    assert S % t == 0 and t % LANES == 0
    return t


def _prep_weights(w_in, mla_w_uq, mla_w_uk):
    D = w_in.shape[0]
    sizes = (DA_COLS, DA_COLS, DA_COLS, DA_COLS, MLA_Q_LORA, MLA_KV_LORA, MLA_ROPE, MLA_WIDTH, 2 * D)
    assert w_in.shape[1] == sum(sizes)
    w_q, w_k, w_v, w_za, w_cq, w_ckv, w_kr, w_zb, w_g = jnp.split(
        w_in.astype(BF16), np.cumsum(sizes)[:-1].tolist(), axis=1)
    w_kr = jnp.pad(w_kr, ((0, 0), (MLA_NOPE, LANES - MLA_QK_DIM)))
    w_nat = jnp.concatenate([w_k, w_v, w_ckv, w_kr], axis=1)
    w_t = jnp.concatenate([w_q, w_cq], axis=1).T
    w_zg = jnp.concatenate([w_za, w_zb, w_g], axis=1)
    pad_heads = lambda w, dh: jnp.pad(
        w.reshape(w.shape[0], MLA_HEADS, dh), ((0, 0), (0, 0), (0, LANES - dh))
    ).reshape(w.shape[0], MLA_QK_COLS).astype(BF16)
    return w_nat, w_t, w_zg, pad_heads(mla_w_uq, MLA_QK_DIM).T, pad_heads(mla_w_uk, MLA_NOPE)


class _LayerWeights(NamedTuple):
    norm_g: jax.Array
    w_nat: jax.Array
    w_t: jax.Array
    w_zg: jax.Array
    gate_b: jax.Array
    da_lambda: jax.Array
    da_head_norm_g: jax.Array
    mla_q_norm_g: jax.Array
    mla_kv_norm_g: jax.Array
    wuqT: jax.Array
    wuk: jax.Array
    wuv: jax.Array
    wa: jax.Array
    wb: jax.Array
    wo: jax.Array
    final_norm_g: jax.Array


def _padded_len(S):
    return -(-S // LANES) * LANES


def _layer(x, past, lam_init, final_norm, w, tables):
    B, S, D = x.shape
    past_len = 0 if past is None else past[0].shape[1]
    Sp = _padded_len(S)
    xp = jnp.pad(x, ((0, 0), (0, Sp - S), (0, 0))) if Sp != S else x
    tm = _tiles(Sp)
    tda, tml = (t[:, past_len:past_len + Sp] for t in tables)

    outs = _inproj(xp, w.norm_g, w.w_nat, w.w_t, w.mla_q_norm_g, w.mla_kv_norm_g, w.wuqT, tda, tml, w.wuk, w.wuv,
                   tm=tm, expand=past is None)
    qT, k_new, v_new, k_b, v_b, lat_new, kr_new, kr32, qmT = outs[:9]
    lat_new = lat_new.reshape(B, Sp, MLA_KV_LORA)

    kv_len = past_len + S
    if past is None:
        k_all, v_all = k_b.reshape(B, Sp, DA_COLS), v_b.reshape(B, Sp, DA_COLS)
        km, vmT = outs[9].reshape(B, Sp, MLA_QK_COLS), outs[10]
        tk = tm
    else:
        pk, pv, pc, pr = past
        tk = -(-kv_len // LANES) * LANES
        cat = lambda old, new: jnp.pad(jnp.concatenate([old, new[:, :S]], axis=1),
                                       ((0, 0), (0, tk - kv_len), (0, 0)))
        k_all = cat(pk.astype(BF16).reshape(B, past_len, DA_COLS), k_b.reshape(B, Sp, DA_COLS))
        v_all = cat(pv.astype(BF16).reshape(B, past_len, DA_COLS), v_b.reshape(B, Sp, DA_COLS))
        c_all = cat(pc, lat_new)
        kr_all = cat(jnp.pad(pr, ((0, 0), (0, 0), (MLA_NOPE, LANES - MLA_QK_DIM))), kr_new.reshape(B, Sp, LANES))
        km, vmT = _expand(c_all, kr_all, w.wuk, w.wuv, tk=tk, kv_len=kv_len)

    o_a = _da_attention(w.da_lambda, w.da_head_norm_g, qT, k_all, v_all,
                        tk=tk, q_off=past_len, kv_len=kv_len, lam_init=lam_init)
    o_b = _mla_attention(qmT, km, vmT, q_off=past_len, kv_len=kv_len)

    n = B * Sp
    y = _epilogue(xp.reshape(n, D), o_a.reshape(n, DA_COLS), o_b.reshape(n, MLA_WIDTH), w.norm_g, w.w_zg,
                  w.gate_b, w.wa, w.wb, w.wo, w.final_norm_g, tm=_tiles(n), final_norm=final_norm)
    y = y.reshape(B, Sp, D)[:, :S]
    new = (k_new[:, :S], v_new[:, :S], lat_new[:, :S], kr32.reshape(B, Sp, MLA_ROPE)[:, :S])
    return y, new


def kernel(x_prompt, x_sample, cache_da_k, cache_da_v, cache_mla_latent, cache_mla_krope, norm_g, w_in, gate_b, da_lambda, da_head_norm_g, mla_q_norm_g, mla_kv_norm_g, mla_w_uq, mla_w_uk, mla_w_uv, w_branch_a, w_branch_b, w_out, final_norm_g):
    depth = w_in.shape[0]
    at = lambda a, l: a.reshape(a.shape[1:]) if depth == 1 else a[l]
    row = lambda v: v.reshape(1, -1)
    hp, hs = x_prompt, x_sample
    n_pos = max(_padded_len(x_prompt.shape[1]), cache_da_k.shape[2] + _padded_len(x_sample.shape[1]))
    pos = jnp.arange(n_pos, dtype=jnp.int32)
    tables = (_rope_tables(pos, DA_ROT, _da_lane), _rope_tables(pos, MLA_ROPE, _mla_lane))
    rows_p, rows_s = [], []
    for l in range(depth):
        lam_init = 0.8 - 0.6 * math.exp(-0.3 * l)
        last = l == depth - 1
        w_nat, w_t, w_zg, wuqT, wuk = _prep_weights(at(w_in, l), at(mla_w_uq, l), at(mla_w_uk, l))
        w = _LayerWeights(
            norm_g=row(at(norm_g, l)), w_nat=w_nat, w_t=w_t, w_zg=w_zg, gate_b=row(at(gate_b, l)),
            da_lambda=at(da_lambda, l), da_head_norm_g=at(da_head_norm_g, l).reshape(-1, 1),
            mla_q_norm_g=at(mla_q_norm_g, l).reshape(-1, 1), mla_kv_norm_g=row(at(mla_kv_norm_g, l)),
            wuqT=wuqT, wuk=wuk, wuv=at(mla_w_uv, l).astype(BF16), wa=at(w_branch_a, l).astype(BF16),
            wb=at(w_branch_b, l).astype(BF16), wo=at(w_out, l).astype(BF16), final_norm_g=row(final_norm_g))
        hp, new_p = _layer(hp, None, lam_init, last, w, tables)
        past = (at(cache_da_k, l), at(cache_da_v, l), at(cache_mla_latent, l), at(cache_mla_krope, l))
        hs, new_s = _layer(hs, past, lam_init, last, w, tables)
        rows_p.append(new_p)
        rows_s.append(new_s)
    stack = lambda rows, i: jnp.stack([r[i] for r in rows], 0)
    return (hp, hs, stack(rows_p, 0), stack(rows_p, 1), stack(rows_p, 2), stack(rows_p, 3),
            stack(rows_s, 0), stack(rows_s, 1), stack(rows_s, 2), stack(rows_s, 3))
```

```python
import functools
import math
from typing import Callable, NamedTuple

import jax
import jax.numpy as jnp
import numpy as np
from jax import lax
from jax.experimental import pallas as pl
from jax.experimental.pallas import tpu as pltpu

CHUNK = 64
ROPE_THETA = 500000.0
RMS_EPS = 1e-6
DA_HEADS = 4
DA_QK_DIM = 64
DA_V_DIM = 2 * DA_QK_DIM
DA_ROT = DA_QK_DIM // 4
DA_COLS = DA_HEADS * DA_V_DIM
MLA_HEADS = 8
MLA_Q_LORA = 384
MLA_KV_LORA = 256
MLA_NOPE = 64
MLA_ROPE = 32
MLA_V_DIM = 64
MLA_WIDTH = MLA_HEADS * MLA_V_DIM
MLA_QK_DIM = MLA_NOPE + MLA_ROPE

LANES = 128
SUM_ROWS = 8
BF16_ROWS = 16
SUM_PAD = BF16_ROWS
MLA_VT_ROWS = 64 + SUM_PAD
MLA_QK_COLS = MLA_HEADS * LANES
MLA_FEAT = LANES - MLA_QK_DIM
LOG2E = math.log2(math.e)
NEG_BIG = -1e30
VMEM_LIMIT_BYTES = 56 * 1024 * 1024
HEAD_PAIR = 2
MLA_GROUP = 2
LOOP_STAGES = 2

F32 = jnp.float32
BF16 = jnp.bfloat16


def _rms(x, g):
    return x * lax.rsqrt(jnp.mean(x * x, axis=-1, keepdims=True) + RMS_EPS) * g


def _rope128(x, tab_ref, shift):
    return (x * tab_ref[0] + pltpu.roll(x, LANES - shift, 1) * tab_ref[1]
            + pltpu.roll(x, shift, 1) * tab_ref[2])


def _rope_tables(pos, rot, lane_of):
    half = rot // 2
    inv = jnp.float32(ROPE_THETA) ** (-jnp.arange(half, dtype=F32) * 2.0 / rot)
    ang = pos.astype(F32)[:, None] * inv
    cos, sin = jnp.cos(ang), jnp.sin(ang)
    idx = np.array([lane_of(j) for j in range(LANES)])
    lo = (idx >= 0) & (idx < half)
    hi = idx >= half
    src = np.where(idx >= 0, idx % half, 0)
    c = jnp.where(jnp.asarray(lo | hi), cos[:, src], 1.0)
    sa = jnp.where(jnp.asarray(lo), -sin[:, src], 0.0)
    sb = jnp.where(jnp.asarray(hi), sin[:, src], 0.0)
    return jnp.stack([c, sa, sb]).astype(F32)


def _da_lane(j):
    c = j % DA_QK_DIM
    return c if c < DA_ROT else -1


def _mla_lane(j):
    c = j - MLA_NOPE
    return c if 0 <= c < MLA_ROPE else -1


def _rope_rows(x, tab_ref, shift):
    up = jnp.concatenate([x[shift:], x[:shift]], axis=0)
    dn = jnp.concatenate([x[-shift:], x[:-shift]], axis=0)
    return x * tab_ref[0] + up * tab_ref[1] + dn * tab_ref[2]


def _inproj_kernel(x_ref, ng_ref, wn_ref, wt_ref, qg_ref, kvg_ref, wuqT_ref, tda_ref, tml_ref, tdaT_ref,
                   tmlT_ref, *refs, q_scale, qm_scale, expand):
    if expand:
        wuk_ref, wuv_ref = refs[:2]
        refs = refs[2:]
    qT_ref, k4_ref, v4_ref, kb_ref, vb_ref, lat_ref, kr_ref, kr32_ref, qmT_ref = refs[:9]
    o_v = DA_COLS
    o_ckv = 2 * DA_COLS
    o_kr = o_ckv + MLA_KV_LORA
    xn = _rms(x_ref[...], ng_ref[...])
    h = xn.astype(BF16)

    proj = jnp.dot(h, wn_ref[...], preferred_element_type=F32)
    for i in range(DA_HEADS):
        lo, hi = i * LANES, (i + 1) * LANES
        k = _rope128(proj[:, lo:hi], tda_ref, DA_ROT // 2)
        v = proj[:, o_v + lo:o_v + hi]
        k4_ref[0, :, i, :] = k
        v4_ref[0, :, i, :] = v
        kb_ref[:, lo:hi] = k.astype(BF16)
        vb_ref[:, lo:hi] = v.astype(BF16)
    lat = _rms(proj[:, o_ckv:o_ckv + MLA_KV_LORA], kvg_ref[...])
    lat_ref[...] = lat
    kr = _rope128(proj[:, o_kr:o_kr + LANES], tml_ref, MLA_ROPE // 2)
    kr_ref[...] = kr
    kr32_ref[...] = kr[:, MLA_NOPE:MLA_QK_DIM]
    if expand:
        _expand_tile(lat, kr, wuk_ref, wuv_ref, refs[9], refs[10], valid_rows=x_ref.shape[0])

    d = xn.shape[1]
    hT = jnp.concatenate([xn[:, c:c + LANES].T.astype(BF16) for c in range(0, d, LANES)], axis=0)
    projT = jnp.dot(wt_ref[...], hT, preferred_element_type=F32)
    for i in range(DA_HEADS):
        lo, hi = i * LANES, (i + 1) * LANES
        qT_ref[0, 0, lo:hi, :] = (_rope_rows(projT[lo:hi], tdaT_ref, DA_ROT // 2) * q_scale).astype(BF16)
    cqT = projT[DA_COLS:]
    qnT = (cqT * lax.rsqrt(jnp.mean(cqT * cqT, axis=0, keepdims=True) + RMS_EPS) * qg_ref[...]).astype(BF16)
    qmT = jnp.dot(wuqT_ref[...], qnT, preferred_element_type=F32)
    for i in range(MLA_HEADS):
        lo, hi = i * LANES, (i + 1) * LANES
        qmT_ref[0, 0, lo:hi, :] = (_rope_rows(qmT[lo:hi], tmlT_ref, MLA_ROPE // 2) * qm_scale).astype(BF16)


def _inproj(x, norm_g, w_nat, w_t, qg_col, kvg, wuqT, tda, tml, wuk, wuv, *, tm, expand):
    B, S, D = x.shape
    n = B * S
    nst = S // tm
    const = lambda i: (0, 0)
    tok = lambda i: (i, 0)
    feat_t = lambda i: (i // nst, i % nst, 0, 0)
    tab = lambda i: (0, i % nst, 0)
    tabT = lambda i: (0, 0, i % nst)
    in_specs = [
        pl.BlockSpec((tm, D), tok),
        pl.BlockSpec((1, D), const),
        pl.BlockSpec(w_nat.shape, const),
        pl.BlockSpec(w_t.shape, const),
        pl.BlockSpec((MLA_Q_LORA, 1), const),
        pl.BlockSpec((1, MLA_KV_LORA), const),
        pl.BlockSpec((MLA_QK_COLS, MLA_Q_LORA), const),
        pl.BlockSpec((3, tm, LANES), tab),
        pl.BlockSpec((3, tm, LANES), tab),
        pl.BlockSpec((3, LANES, tm), tabT),
        pl.BlockSpec((3, LANES, tm), tabT),
    ]
    out_specs = [
        pl.BlockSpec((1, 1, DA_COLS, tm), feat_t),
        pl.BlockSpec((1, tm, DA_HEADS, DA_V_DIM), feat_t),
        pl.BlockSpec((1, tm, DA_HEADS, DA_V_DIM), feat_t),
        pl.BlockSpec((tm, DA_COLS), tok),
        pl.BlockSpec((tm, DA_COLS), tok),
        pl.BlockSpec((tm, MLA_KV_LORA), tok),
        pl.BlockSpec((tm, LANES), tok),
        pl.BlockSpec((tm, MLA_ROPE), tok),
        pl.BlockSpec((1, 1, MLA_QK_COLS, tm), feat_t),
    ]
    out_shape = [
        jax.ShapeDtypeStruct((B, nst, DA_COLS, tm), BF16),
        jax.ShapeDtypeStruct((B, S, DA_HEADS, DA_V_DIM), F32),
        jax.ShapeDtypeStruct((B, S, DA_HEADS, DA_V_DIM), F32),
        jax.ShapeDtypeStruct((n, DA_COLS), BF16),
        jax.ShapeDtypeStruct((n, DA_COLS), BF16),
        jax.ShapeDtypeStruct((n, MLA_KV_LORA), F32),
        jax.ShapeDtypeStruct((n, LANES), F32),
        jax.ShapeDtypeStruct((n, MLA_ROPE), F32),
        jax.ShapeDtypeStruct((B, nst, MLA_QK_COLS, tm), BF16),
    ]
    args = [x.reshape(n, D), norm_g, w_nat, w_t, qg_col, kvg, wuqT, tda, tml,
            tda.transpose(0, 2, 1), tml.transpose(0, 2, 1)]
    if expand:
        assert _n_features(tm) <= MLA_FEAT
        in_specs += [pl.BlockSpec(wuk.shape, const), pl.BlockSpec(wuv.shape, const)]
        out_specs += [pl.BlockSpec((tm, MLA_QK_COLS), tok),
                      pl.BlockSpec((1, 1, MLA_HEADS * MLA_VT_ROWS, tm), feat_t)]
        out_shape += [jax.ShapeDtypeStruct((n, MLA_QK_COLS), BF16),
                      jax.ShapeDtypeStruct((B, nst, MLA_HEADS * MLA_VT_ROWS, tm), BF16)]
        args += [wuk, wuv]
    return pl.pallas_call(
        functools.partial(_inproj_kernel, q_scale=DA_QK_DIM ** -0.5 * LOG2E,
                          qm_scale=MLA_QK_DIM ** -0.5 * LOG2E, expand=expand),
        grid=(n // tm,),
        in_specs=in_specs,
        out_specs=out_specs,
        out_shape=out_shape,
        compiler_params=pltpu.CompilerParams(
            dimension_semantics=("arbitrary",), vmem_limit_bytes=VMEM_LIMIT_BYTES),
        name="inproj",
    )(*args)


def _n_features(tk):
    return tk // CHUNK + 1


def _key_features(tk, valid_rows, lane0, dtype):
    nf = _n_features(tk)
    row = lax.broadcasted_iota(jnp.int32, (tk, LANES), 0)
    f = lax.broadcasted_iota(jnp.int32, (tk, LANES), 1) - lane0
    chunk_hit = (f >= 0) & (f < nf - 1) & (lax.shift_right_logical(row, 6) == f)
    pad_hit = (f == nf - 1) & (row >= valid_rows)
    return jnp.where(chunk_hit | pad_hit, 1.0, 0.0).astype(dtype)


def _query_coeffs(nrows, ncols, tk, tq, q_rel, diag):
    nf = _n_features(tk)
    r = lax.broadcasted_iota(jnp.int32, (nrows, ncols), 0)
    col = lax.broadcasted_iota(jnp.int32, (nrows, ncols), 1)
    q_chunk = lax.shift_right_logical(q_rel + jnp.where(col >= tq, col - tq, col), 6)
    hidden = (r == nf - 1)
    if diag:
        hidden = hidden | ((r < nf - 1) & (r > q_chunk))
    return jnp.where(hidden, NEG_BIG, 0.0).astype(BF16)


def _expand_tile(lat, kr, wuk_ref, wuv_ref, km_ref, vT_ref, *, valid_rows):
    tk = lat.shape[0]
    c = lat.astype(BF16)
    kn = jnp.dot(c, wuk_ref[...], preferred_element_type=F32)
    tail = kr + _key_features(tk, valid_rows, MLA_QK_DIM, F32)
    for i in range(MLA_HEADS):
        lo, hi = i * LANES, (i + 1) * LANES
        km_ref[:, lo:hi] = (kn[:, lo:hi] + tail).astype(BF16)
    vm = jnp.dot(c, wuv_ref[...], preferred_element_type=F32)
    ones = _ones_rows(tk)
    for i in range(MLA_WIDTH // LANES):
        vt = vm[:, i * LANES:(i + 1) * LANES].T.astype(BF16)
        for hh in range(LANES // MLA_V_DIM):
            r0 = (i * (LANES // MLA_V_DIM) + hh) * MLA_VT_ROWS
            vT_ref[0, 0, r0:r0 + MLA_V_DIM, :] = vt[hh * MLA_V_DIM:(hh + 1) * MLA_V_DIM]
            vT_ref[0, 0, r0 + MLA_V_DIM:r0 + MLA_VT_ROWS, :] = ones


def _expand_kernel(c_ref, kr_ref, wuk_ref, wuv_ref, km_ref, vT_ref, *, valid_rows):
    _expand_tile(c_ref[0], kr_ref[0], wuk_ref, wuv_ref, km_ref.at[0], vT_ref, valid_rows=valid_rows)


def _expand(c, kr, wuk, wuv, *, tk, kv_len):
    B, T, _ = c.shape
    nkv = T // tk
    assert _n_features(tk) <= MLA_FEAT
    const = lambda b, j: (0, 0)
    tile = lambda b, j: (b, j, 0)
    return pl.pallas_call(
        functools.partial(_expand_kernel, valid_rows=kv_len - (nkv - 1) * tk),
        grid=(B, nkv),
        in_specs=[
            pl.BlockSpec((1, tk, MLA_KV_LORA), tile),
            pl.BlockSpec((1, tk, LANES), tile),
            pl.BlockSpec((MLA_KV_LORA, MLA_QK_COLS), const),
            pl.BlockSpec((MLA_KV_LORA, MLA_WIDTH), const),
        ],
        out_specs=[
            pl.BlockSpec((1, tk, MLA_QK_COLS), tile),
            pl.BlockSpec((1, 1, MLA_HEADS * MLA_VT_ROWS, tk), lambda b, j: (b, j, 0, 0)),
        ],
        out_shape=[
            jax.ShapeDtypeStruct((B, T, MLA_QK_COLS), BF16),
            jax.ShapeDtypeStruct((B, nkv, MLA_HEADS * MLA_VT_ROWS, tk), BF16),
        ],
        compiler_params=pltpu.CompilerParams(
            dimension_semantics=("arbitrary", "arbitrary"), vmem_limit_bytes=VMEM_LIMIT_BYTES),
        name="kv_expand",
    )(c, kr, wuk, wuv)


def _check_tiling(*, nq, tq, tk, nkv, q_off, kv_len):
    single = nq == 1 and nkv == 1
    causal = q_off == 0 and tq == tk and nq == nkv and kv_len == nkv * tk and (nq * (nq + 1) // 2) % 2 == 0
    assert single or causal, (nq, tq, tk, nkv, q_off, kv_len)


class _Chain(NamedTuple):
    load_k: Callable
    load_vT: Callable
    load_q: Callable
    s: object
    mx: object
    p: object
    a: object
    m: object
    acc: object


def _flash_flat(chains, nq, finalize):
    for c in chains:
        tk = c.s.shape[1]
        c.m[...] = jnp.full(c.m.shape, NEG_BIG, F32)
        c.acc[...] = jnp.zeros(c.acc.shape, F32)

    def advance(u):
        qi, j = u
        last = j == qi
        return jnp.where(last, qi + 1, qi), jnp.where(last, 0, j + 1)

    def scores(u, slot):
        qi, j = u
        for c in chains:
            s = jnp.dot(c.load_k(j), c.load_q(qi, j == qi), preferred_element_type=F32)
            c.s[slot] = s
            c.mx[slot] = jnp.max(s, axis=0, keepdims=True)

    def softmax(u, slot):
        qi, j = u
        for c in chains:
            m_prev = jnp.where(j == 0, NEG_BIG, c.m[...])
            m_new = jnp.maximum(m_prev, c.mx[slot])
            c.m[...] = m_new
            c.a[slot] = jnp.exp2(m_prev - m_new)
            for r in range(0, tk, BF16_ROWS):
                x = (c.s[slot, r:r + BF16_ROWS, :] - m_new).astype(BF16)
                c.p[slot, r:r + BF16_ROWS, :] = jnp.exp2(x)

    def values(u, slot):
        qi, j = u
        par = lax.rem(qi, 2)
        for c in chains:
            for vT, cs in c.load_vT(j):
                pv = jnp.dot(vT, c.p[slot, :, cs], preferred_element_type=F32)
                c.acc[par, :, cs] = c.a[slot, :, cs] * c.acc[par, :, cs] + pv

    def stages(ua, count):
        units = [ua]
        for _ in range(count + 1):
            units.append(advance(units[-1]))
        for i in range(count):
            cur, nxt = (i + 1) % 2, i % 2
            scores(units[i + 2], nxt)
            softmax(units[i + 1], cur)
            values(units[i], nxt)
        for i in range(0, count, 2):
            done_a = units[i][1] == units[i][0]
            done_b = units[i + 1][1] == units[i + 1][0]

            @pl.when(jnp.logical_or(done_a, done_b))
            def _():
                finalize(jnp.where(done_a, units[i][0], units[i + 1][0]))

        return units[count]

    zero = jnp.int32(0)
    u0 = (zero, zero)
    n_units = nq * (nq + 1) // 2
    if n_units == 1:
        scores(u0, 0)
        softmax(u0, 0)
        values(u0, 0)
        finalize(0)
        return

    scores(u0, 0)
    scores(advance(u0), 1)
    softmax(u0, 0)

    ua = lax.fori_loop(0, (n_units - 2) // LOOP_STAGES, lambda t, ua: stages(ua, LOOP_STAGES), u0)
    rest = (n_units - 2) % LOOP_STAGES
    if rest:
        ua = stages(ua, rest)
    ub = advance(ua)
    softmax(ub, 1)
    values(ua, 0)
    values(ub, 1)
    finalize(nq - 1)


def _ones_rows(tk):
    row = lax.broadcasted_iota(jnp.int32, (SUM_PAD, tk), 0)
    return jnp.where(row == 0, 1.0, 0.0).astype(BF16)


def _da_attn_kernel(lam_ref, hg_ref, qT_ref, k_ref, v_ref, o_ref,
                    vT_ref, qs_ref, feat_ref, gv_ref, s_ref, mx_ref, p_ref, a_ref, m_ref, acc_ref,
                    *, nq, tq, tk, nkv, q_off, kv_len, lam_init):
    valid_rows = kv_len - (nkv - 1) * tk
    feat_ref[...] = _key_features(tk, valid_rows, 0, BF16)
    q_rel = q_off if nkv == 1 else 0
    gv_ref[0] = _query_coeffs(LANES, 2 * tq, tk, tq, q_rel, diag=False)
    gv_ref[1] = _query_coeffs(LANES, 2 * tq, tk, tq, q_rel, diag=True)

    row = lax.broadcasted_iota(jnp.int32, (LANES, tq), 0)
    zero = jnp.zeros((LANES, tq), BF16)
    chains = []
    for hh in range(HEAD_PAIR):
        cols = slice(hh * LANES, (hh + 1) * LANES)
        for c in range(nkv):
            vT_ref[hh, c, :DA_V_DIM] = v_ref[0, c * tk:(c + 1) * tk, cols].astype(F32).T.astype(BF16)
            vT_ref[hh, c, DA_V_DIM:] = _ones_rows(tk)
        for qi in range(nq):
            qT = qT_ref[0, qi, cols, :]
            qs_ref[hh, qi, :, :tq] = jnp.where(row < DA_QK_DIM, qT, zero)
            qs_ref[hh, qi, :, tq:] = jnp.where(row >= DA_QK_DIM, qT, zero)
        chains.append(_Chain(
            load_k=lambda j, cols=cols: jnp.concatenate(
                [k_ref[0, pl.ds(pl.multiple_of(j * tk, tk), tk), cols], feat_ref[...]], axis=1),
            load_vT=lambda j, hh=hh: [(vT_ref[hh, j], slice(None))],
            load_q=lambda qi, diag, hh=hh: jnp.concatenate(
                [qs_ref[hh, qi], gv_ref[diag.astype(jnp.int32)]], axis=0),
            s=s_ref.at[hh], mx=mx_ref.at[hh], p=p_ref.at[hh], a=a_ref.at[hh], m=m_ref.at[hh],
            acc=acc_ref.at[hh]))

    lf = lam_ref[...]
    lam = (jnp.exp(jnp.sum(lf[0:1] * lf[1:2], axis=-1, keepdims=True))
           - jnp.exp(jnp.sum(lf[2:3] * lf[3:4], axis=-1, keepdims=True)) + lam_init)

    def finalize(qi):
        par = qi % 2
        for hh in range(HEAD_PAIR):
            acc = acc_ref[hh, par]
            o = acc[:DA_V_DIM] * (1.0 / acc[DA_V_DIM:DA_V_DIM + 1])
            o = o[:, :tq] - lam * o[:, tq:]
            y = o * lax.rsqrt(jnp.mean(o * o, axis=0, keepdims=True) + RMS_EPS) * hg_ref[...] * (1.0 - lam_init)
            o_ref[0, qi, hh * LANES:(hh + 1) * LANES, :] = y.astype(BF16)

    _flash_flat(chains, nq, finalize)


def _da_attention(lam, hg, qT, k, v, *, tk, q_off, kv_len, lam_init):
    B, nq, _, tq = qT.shape
    T = k.shape[1]
    nkv = T // tk
    _check_tiling(nq=nq, tq=tq, tk=tk, nkv=nkv, q_off=q_off, kv_len=kv_len)
    assert _n_features(tk) <= LANES
    pair = HEAD_PAIR * LANES
    head_kv = lambda b, h: (b, 0, h)
    return pl.pallas_call(
        functools.partial(_da_attn_kernel, nq=nq, tq=tq, tk=tk, nkv=nkv, q_off=q_off, kv_len=kv_len,
                          lam_init=lam_init),
        grid=(B, DA_HEADS // HEAD_PAIR),
        in_specs=[
            pl.BlockSpec(lam.shape, lambda b, h: (0, 0)),
            pl.BlockSpec((DA_V_DIM, 1), lambda b, h: (0, 0)),
            pl.BlockSpec((1, nq, pair, tq), lambda b, h: (b, 0, h, 0)),
            pl.BlockSpec((1, T, pair), head_kv),
            pl.BlockSpec((1, T, pair), head_kv),
        ],
        out_specs=pl.BlockSpec((1, nq, pair, tq), lambda b, h: (b, 0, h, 0)),
        out_shape=jax.ShapeDtypeStruct((B, nq, DA_COLS, tq), BF16),
        scratch_shapes=[
            pltpu.VMEM((HEAD_PAIR, nkv, DA_V_DIM + SUM_PAD, tk), BF16),
            pltpu.VMEM((HEAD_PAIR, nq, LANES, 2 * tq), BF16),
            pltpu.VMEM((tk, LANES), BF16),
            pltpu.VMEM((2, LANES, 2 * tq), BF16),
            pltpu.VMEM((HEAD_PAIR, 2, tk, 2 * tq), F32),
            pltpu.VMEM((HEAD_PAIR, 2, 1, 2 * tq), F32),
            pltpu.VMEM((HEAD_PAIR, 2, tk, 2 * tq), BF16),
            pltpu.VMEM((HEAD_PAIR, 2, 1, 2 * tq), F32),
            pltpu.VMEM((HEAD_PAIR, 1, 2 * tq), F32),
            pltpu.VMEM((HEAD_PAIR, 2, DA_V_DIM + SUM_PAD, 2 * tq), F32),
        ],
        compiler_params=pltpu.CompilerParams(
            dimension_semantics=("arbitrary", "arbitrary"), vmem_limit_bytes=VMEM_LIMIT_BYTES),
        name="da_attention",
    )(lam, hg, qT, k, v)


def _mla_attn_kernel(qT_ref, k_ref, vT_ref, o_ref, gv_ref, s_ref, mx_ref, p_ref, a_ref, m_ref, acc_ref,
                     *, nq, tq, tk, nkv, q_off):
    q_rel = q_off if nkv == 1 else 0
    gv_ref[0] = _query_coeffs(MLA_FEAT, tq, tk, tq, q_rel, diag=False)
    gv_ref[1] = _query_coeffs(MLA_FEAT, tq, tk, tq, q_rel, diag=True)
    zero = jnp.zeros((LANES, tq), BF16)
    halves = (slice(0, tq), slice(tq, 2 * tq))
    chains = []
    for cc in range(MLA_GROUP):
        h0 = 2 * cc

        def load_q(qi, diag, h0=h0):
            gv = gv_ref[diag.astype(jnp.int32)]
            qa, qb = (jnp.concatenate([qT_ref[0, qi, h * LANES:h * LANES + MLA_QK_DIM, :], gv], axis=0)
                      for h in (h0, h0 + 1))
            return jnp.concatenate([jnp.concatenate([qa, zero], axis=1),
                                    jnp.concatenate([zero, qb], axis=1)], axis=0)

        chains.append(_Chain(
            load_k=lambda j, h0=h0: k_ref[0, pl.ds(pl.multiple_of(j * tk, tk), tk),
                                          h0 * LANES:(h0 + 2) * LANES],
            load_vT=lambda j, h0=h0: [
                (vT_ref[0, j, (h0 + i) * MLA_VT_ROWS:(h0 + i + 1) * MLA_VT_ROWS, :], halves[i]) for i in range(2)],
            load_q=load_q,
            s=s_ref.at[cc], mx=mx_ref.at[cc], p=p_ref.at[cc], a=a_ref.at[cc], m=m_ref.at[cc],
            acc=acc_ref.at[cc]))

    def finalize(qi):
        par = qi % 2
        for cc in range(MLA_GROUP):
            acc = acc_ref[cc, par]
            o = acc[:MLA_V_DIM] * (1.0 / acc[MLA_V_DIM:MLA_V_DIM + 1])
            o_ref[0, qi, cc * LANES:(cc + 1) * LANES, :] = jnp.concatenate(
                [o[:, halves[0]], o[:, halves[1]]], axis=0).astype(BF16)

    _flash_flat(chains, nq, finalize)


def _mla_attention(qmT, km, vT, *, q_off, kv_len):
    B, nq, _, tq = qmT.shape
    nkv, tk = vT.shape[1], vT.shape[3]
    T = km.shape[1]
    _check_tiling(nq=nq, tq=tq, tk=tk, nkv=nkv, q_off=q_off, kv_len=kv_len)
    heads = 2 * MLA_GROUP
    ncols = 2 * tq
    return pl.pallas_call(
        functools.partial(_mla_attn_kernel, nq=nq, tq=tq, tk=tk, nkv=nkv, q_off=q_off),
        grid=(B, MLA_HEADS // heads),
        in_specs=[
            pl.BlockSpec((1, nq, heads * LANES, tq), lambda b, h: (b, 0, h, 0)),
            pl.BlockSpec((1, T, heads * LANES), lambda b, h: (b, 0, h)),
            pl.BlockSpec((1, nkv, heads * MLA_VT_ROWS, tk), lambda b, h: (b, 0, h, 0)),
        ],
        out_specs=pl.BlockSpec((1, nq, heads * MLA_V_DIM, tq), lambda b, h: (b, 0, h, 0)),
        out_shape=jax.ShapeDtypeStruct((B, nq, MLA_WIDTH, tq), BF16),
        scratch_shapes=[
            pltpu.VMEM((2, MLA_FEAT, tq), BF16),
            pltpu.VMEM((MLA_GROUP, 2, tk, ncols), F32),
            pltpu.VMEM((MLA_GROUP, 2, 1, ncols), F32),
            pltpu.VMEM((MLA_GROUP, 2, tk, ncols), BF16),
            pltpu.VMEM((MLA_GROUP, 2, 1, ncols), F32),
            pltpu.VMEM((MLA_GROUP, 1, ncols), F32),
            pltpu.VMEM((MLA_GROUP, 2, MLA_VT_ROWS, ncols), F32),
        ],
        compiler_params=pltpu.CompilerParams(
            dimension_semantics=("arbitrary", "arbitrary"), vmem_limit_bytes=VMEM_LIMIT_BYTES),
        name="mla_attention",
    )(qmT, km, vT)


def _epilogue_kernel(x_ref, oa_ref, ob_ref, ng_ref, wzg_ref, gb_ref, wa_ref, wb_ref, wo_ref, fg_ref,
                     y_ref, *, final_norm):
    x = x_ref[...]
    d = x.shape[1]
    h = _rms(x, ng_ref[...]).astype(BF16)
    zg = jnp.dot(h, wzg_ref[...], preferred_element_type=F32)
    untranspose = lambda ref: jnp.concatenate(
        [ref[0, 0, c:c + LANES, :].astype(F32).T for c in range(0, ref.shape[2], LANES)], axis=1)
    ga = (untranspose(oa_ref) * jax.nn.silu(zg[:, :DA_COLS])).astype(BF16)
    gb = (untranspose(ob_ref) * jax.nn.silu(zg[:, DA_COLS:DA_COLS + MLA_WIDTH])).astype(BF16)
    ya = jnp.dot(ga, wa_ref[...], preferred_element_type=F32)
    yb = jnp.dot(gb, wb_ref[...], preferred_element_type=F32)
    g = jax.nn.sigmoid(zg[:, DA_COLS + MLA_WIDTH:] + gb_ref[...])
    m = (g[:, :d] * ya + g[:, d:] * yb).astype(BF16)
    out = x + jnp.dot(m, wo_ref[...], preferred_element_type=F32)
    y_ref[...] = _rms(out, fg_ref[...]) if final_norm else out


def _epilogue(x, oaT, obT, norm_g, wzg, gate_b, wa, wb, wo, final_g, *, final_norm):
    n, D = x.shape
    _, nst, _, tm = oaT.shape
    const = lambda i: (0, 0)
    tok = lambda i: (i, 0)
    tile_t = lambda i: (i // nst, i % nst, 0, 0)
    return pl.pallas_call(
        functools.partial(_epilogue_kernel, final_norm=final_norm),
        grid=(n // tm,),
        in_specs=[
            pl.BlockSpec((tm, D), tok),
            pl.BlockSpec((1, 1, DA_COLS, tm), tile_t),
            pl.BlockSpec((1, 1, MLA_WIDTH, tm), tile_t),
            pl.BlockSpec((1, D), const),
            pl.BlockSpec(wzg.shape, const),
            pl.BlockSpec((1, 2 * D), const),
            pl.BlockSpec(wa.shape, const),
            pl.BlockSpec(wb.shape, const),
            pl.BlockSpec(wo.shape, const),
            pl.BlockSpec((1, D), const),
        ],
        out_specs=pl.BlockSpec((tm, D), tok),
        out_shape=jax.ShapeDtypeStruct((n, D), F32),
        compiler_params=pltpu.CompilerParams(
            dimension_semantics=("arbitrary",), vmem_limit_bytes=VMEM_LIMIT_BYTES),
        name="epilogue",
    )(x, oaT, obT, norm_g, wzg, gate_b, wa, wb, wo, final_g)


def _tiles(S):
    t = min(512, S)
    assert S % t == 0 and t % LANES == 0
    return t


def _prep_weights(w_in, mla_w_uq, mla_w_uk):
    D = w_in.shape[0]
    sizes = (DA_COLS, DA_COLS, DA_COLS, DA_COLS, MLA_Q_LORA, MLA_KV_LORA, MLA_ROPE, MLA_WIDTH, 2 * D)
    assert w_in.shape[1] == sum(sizes)
    w_q, w_k, w_v, w_za, w_cq, w_ckv, w_kr, w_zb, w_g = jnp.split(w_in, np.cumsum(sizes)[:-1].tolist(), axis=1)
    w_kr = jnp.pad(w_kr, ((0, 0), (MLA_NOPE, LANES - MLA_QK_DIM)))
    w_nat = jnp.concatenate([w_k, w_v, w_ckv, w_kr], axis=1).astype(BF16)
    w_t = jnp.concatenate([w_q, w_cq], axis=1).T.astype(BF16)
    w_zg = jnp.concatenate([w_za, w_zb, w_g], axis=1).astype(BF16)
    pad_heads = lambda w, dh: jnp.pad(
        w.reshape(w.shape[0], MLA_HEADS, dh), ((0, 0), (0, 0), (0, LANES - dh))
    ).reshape(w.shape[0], MLA_QK_COLS).astype(BF16)
    return w_nat, w_t, w_zg, pad_heads(mla_w_uq, MLA_QK_DIM).T, pad_heads(mla_w_uk, MLA_NOPE)


class _LayerWeights(NamedTuple):
    norm_g: jax.Array
    w_nat: jax.Array
    w_t: jax.Array
    w_zg: jax.Array
    gate_b: jax.Array
    da_lambda: jax.Array
    da_head_norm_g: jax.Array
    mla_q_norm_g: jax.Array
    mla_kv_norm_g: jax.Array
    wuqT: jax.Array
    wuk: jax.Array
    wuv: jax.Array
    wa: jax.Array
    wb: jax.Array
    wo: jax.Array
    final_norm_g: jax.Array


def _layer(x, past, lam_init, final_norm, w):
    B, S, D = x.shape
    past_len = 0 if past is None else past[0].shape[1]
    Sp = -(-S // LANES) * LANES
    xp = jnp.pad(x, ((0, 0), (0, Sp - S), (0, 0))) if Sp != S else x
    tm = _tiles(Sp)
    pos = past_len + jnp.arange(Sp, dtype=jnp.int32)
    tda = _rope_tables(pos, DA_ROT, _da_lane)
    tml = _rope_tables(pos, MLA_ROPE, _mla_lane)

    outs = _inproj(xp, w.norm_g, w.w_nat, w.w_t, w.mla_q_norm_g, w.mla_kv_norm_g, w.wuqT, tda, tml, w.wuk, w.wuv,
                   tm=tm, expand=past is None)
    qT, k_new, v_new, k_b, v_b, lat_new, kr_new, kr32, qmT = outs[:9]
    lat_new = lat_new.reshape(B, Sp, MLA_KV_LORA)

    kv_len = past_len + S
    if past is None:
        k_all, v_all = k_b.reshape(B, Sp, DA_COLS), v_b.reshape(B, Sp, DA_COLS)
        km, vmT = outs[9].reshape(B, Sp, MLA_QK_COLS), outs[10]
        tk = tm
    else:
        pk, pv, pc, pr = past
        tk = -(-kv_len // LANES) * LANES
        cat = lambda old, new: jnp.pad(jnp.concatenate([old, new[:, :S]], axis=1),
                                       ((0, 0), (0, tk - kv_len), (0, 0)))
        k_all = cat(pk.reshape(B, past_len, DA_COLS), k_new.reshape(B, Sp, DA_COLS)).astype(BF16)
        v_all = cat(pv.reshape(B, past_len, DA_COLS), v_new.reshape(B, Sp, DA_COLS)).astype(BF16)
        c_all = cat(pc, lat_new)
        kr_all = cat(jnp.pad(pr, ((0, 0), (0, 0), (MLA_NOPE, LANES - MLA_QK_DIM))), kr_new.reshape(B, Sp, LANES))
        km, vmT = _expand(c_all, kr_all, w.wuk, w.wuv, tk=tk, kv_len=kv_len)

    o_a = _da_attention(w.da_lambda, w.da_head_norm_g, qT, k_all, v_all,
                        tk=tk, q_off=past_len, kv_len=kv_len, lam_init=lam_init)
    o_b = _mla_attention(qmT, km, vmT, q_off=past_len, kv_len=kv_len)

    n = B * Sp
    y = _epilogue(xp.reshape(n, D), o_a, o_b, w.norm_g, w.w_zg,
                  w.gate_b, w.wa, w.wb, w.wo, w.final_norm_g, final_norm=final_norm)
    y = y.reshape(B, Sp, D)[:, :S]
    new = (k_new[:, :S], v_new[:, :S], lat_new[:, :S], kr32.reshape(B, Sp, MLA_ROPE)[:, :S])
    return y, new


def kernel(x_prompt, x_sample, cache_da_k, cache_da_v, cache_mla_latent, cache_mla_krope, norm_g, w_in, gate_b, da_lambda, da_head_norm_g, mla_q_norm_g, mla_kv_norm_g, mla_w_uq, mla_w_uk, mla_w_uv, w_branch_a, w_branch_b, w_out, final_norm_g):
    depth = w_in.shape[0]
    at = lambda a, l: a.reshape(a.shape[1:]) if depth == 1 else a[l]
    row = lambda v: v.reshape(1, -1)
    hp, hs = x_prompt, x_sample
    rows_p, rows_s = [], []
    for l in range(depth):
        lam_init = 0.8 - 0.6 * math.exp(-0.3 * l)
        last = l == depth - 1
        w_nat, w_t, w_zg, wuqT, wuk = _prep_weights(at(w_in, l), at(mla_w_uq, l), at(mla_w_uk, l))
        w = _LayerWeights(
            norm_g=row(at(norm_g, l)), w_nat=w_nat, w_t=w_t, w_zg=w_zg, gate_b=row(at(gate_b, l)),
            da_lambda=at(da_lambda, l), da_head_norm_g=at(da_head_norm_g, l).reshape(-1, 1),
            mla_q_norm_g=at(mla_q_norm_g, l).reshape(-1, 1), mla_kv_norm_g=row(at(mla_kv_norm_g, l)),
            wuqT=wuqT, wuk=wuk, wuv=at(mla_w_uv, l).astype(BF16), wa=at(w_branch_a, l).astype(BF16),
            wb=at(w_branch_b, l).astype(BF16), wo=at(w_out, l).astype(BF16), final_norm_g=row(final_norm_g))
        hp, new_p = _layer(hp, None, lam_init, last, w)
        past = (at(cache_da_k, l), at(cache_da_v, l), at(cache_mla_latent, l), at(cache_mla_krope, l))
        hs, new_s = _layer(hs, past, lam_init, last, w)
        rows_p.append(new_p)
        rows_s.append(new_s)
    stack = lambda rows, i: jnp.stack([r[i] for r in rows], 0)
    return (hp, hs, stack(rows_p, 0), stack(rows_p, 1), stack(rows_p, 2), stack(rows_p, 3),
            stack(rows_s, 0), stack(rows_s, 1), stack(rows_s, 2), stack(rows_s, 3))
```

```python
import functools
import math
from typing import Callable, NamedTuple

import jax
import jax.numpy as jnp
import numpy as np
from jax import lax
from jax.experimental import pallas as pl
from jax.experimental.pallas import tpu as pltpu

CHUNK = 64
ROPE_THETA = 500000.0
RMS_EPS = 1e-6
DA_HEADS = 4
DA_QK_DIM = 64
DA_V_DIM = 2 * DA_QK_DIM
DA_ROT = DA_QK_DIM // 4
DA_COLS = DA_HEADS * DA_V_DIM
MLA_HEADS = 8
MLA_Q_LORA = 384
MLA_KV_LORA = 256
MLA_NOPE = 64
MLA_ROPE = 32
MLA_V_DIM = 64
MLA_WIDTH = MLA_HEADS * MLA_V_DIM
MLA_QK_DIM = MLA_NOPE + MLA_ROPE

LANES = 128
SUM_ROWS = 8
BF16_ROWS = 16
SUM_PAD = BF16_ROWS
MLA_VT_ROWS = 64 + SUM_PAD
MLA_QK_COLS = MLA_HEADS * LANES
MLA_FEAT = LANES - MLA_QK_DIM
LOG2E = math.log2(math.e)
NEG_BIG = -1e30
VMEM_LIMIT_BYTES = 56 * 1024 * 1024
HEAD_PAIR = 2
MLA_GROUP = 2
LOOP_STAGES = 2

F32 = jnp.float32
BF16 = jnp.bfloat16


def _rms(x, g):
    return x * lax.rsqrt(jnp.mean(x * x, axis=-1, keepdims=True) + RMS_EPS) * g


def _rope128(x, tab_ref, shift):
    return (x * tab_ref[0] + pltpu.roll(x, LANES - shift, 1) * tab_ref[1]
            + pltpu.roll(x, shift, 1) * tab_ref[2])


def _rope_tables(pos, rot, lane_of):
    half = rot // 2
    inv = jnp.float32(ROPE_THETA) ** (-jnp.arange(half, dtype=F32) * 2.0 / rot)
    ang = pos.astype(F32)[:, None] * inv
    cos, sin = jnp.cos(ang), jnp.sin(ang)
    idx = np.array([lane_of(j) for j in range(LANES)])
    lo = (idx >= 0) & (idx < half)
    hi = idx >= half
    src = np.where(idx >= 0, idx % half, 0)
    c = jnp.where(jnp.asarray(lo | hi), cos[:, src], 1.0)
    sa = jnp.where(jnp.asarray(lo), -sin[:, src], 0.0)
    sb = jnp.where(jnp.asarray(hi), sin[:, src], 0.0)
    return jnp.stack([c, sa, sb]).astype(F32)


def _da_lane(j):
    c = j % DA_QK_DIM
    return c if c < DA_ROT else -1


def _mla_lane(j):
    c = j - MLA_NOPE
    return c if 0 <= c < MLA_ROPE else -1


def _rope_rows(x, tab_ref, shift):
    up = jnp.concatenate([x[shift:], x[:shift]], axis=0)
    dn = jnp.concatenate([x[-shift:], x[:-shift]], axis=0)
    return x * tab_ref[0] + up * tab_ref[1] + dn * tab_ref[2]


def _inproj_kernel(x_ref, ng_ref, wn_ref, wt_ref, qg_ref, kvg_ref, wuqT_ref, tda_ref, tml_ref, tdaT_ref,
                   tmlT_ref, *refs, q_scale, qm_scale, expand):
    if expand:
        wuk_ref, wuv_ref = refs[:2]
        refs = refs[2:]
    qT_ref, k4_ref, v4_ref, kb_ref, vb_ref, lat_ref, kr_ref, kr32_ref, qmT_ref = refs[:9]
    o_v = DA_COLS
    o_ckv = 2 * DA_COLS
    o_kr = o_ckv + MLA_KV_LORA
    xn = _rms(x_ref[...], ng_ref[...])
    h = xn.astype(BF16)

    proj = jnp.dot(h, wn_ref[...], preferred_element_type=F32)
    for i in range(DA_HEADS):
        lo, hi = i * LANES, (i + 1) * LANES
        k = _rope128(proj[:, lo:hi], tda_ref, DA_ROT // 2)
        v = proj[:, o_v + lo:o_v + hi]
        k4_ref[0, :, i, :] = k
        v4_ref[0, :, i, :] = v
        kb_ref[:, lo:hi] = k.astype(BF16)
        vb_ref[:, lo:hi] = v.astype(BF16)
    lat = _rms(proj[:, o_ckv:o_ckv + MLA_KV_LORA], kvg_ref[...])
    lat_ref[...] = lat
    kr = _rope128(proj[:, o_kr:o_kr + LANES], tml_ref, MLA_ROPE // 2)
    kr_ref[...] = kr
    kr32_ref[...] = kr[:, MLA_NOPE:MLA_QK_DIM]
    if expand:
        _expand_tile(lat, kr, wuk_ref, wuv_ref, refs[9], refs[10], valid_rows=x_ref.shape[0])

    d = xn.shape[1]
    hT = jnp.concatenate([xn[:, c:c + LANES].T.astype(BF16) for c in range(0, d, LANES)], axis=0)
    projT = jnp.dot(wt_ref[...], hT, preferred_element_type=F32)
    for i in range(DA_HEADS):
        lo, hi = i * LANES, (i + 1) * LANES
        qT_ref[0, 0, lo:hi, :] = (_rope_rows(projT[lo:hi], tdaT_ref, DA_ROT // 2) * q_scale).astype(BF16)
    cqT = projT[DA_COLS:]
    qnT = (cqT * lax.rsqrt(jnp.mean(cqT * cqT, axis=0, keepdims=True) + RMS_EPS) * qg_ref[...]).astype(BF16)
    qmT = jnp.dot(wuqT_ref[...], qnT, preferred_element_type=F32)
    for i in range(MLA_HEADS):
        lo, hi = i * LANES, (i + 1) * LANES
        qmT_ref[0, 0, lo:hi, :] = (_rope_rows(qmT[lo:hi], tmlT_ref, MLA_ROPE // 2) * qm_scale).astype(BF16)


def _inproj(x, norm_g, w_nat, w_t, qg_col, kvg, wuqT, tda, tml, wuk, wuv, *, tm, expand):
    B, S, D = x.shape
    n = B * S
    nst = S // tm
    const = lambda i: (0, 0)
    tok = lambda i: (i, 0)
    feat_t = lambda i: (i // nst, i % nst, 0, 0)
    tab = lambda i: (0, i % nst, 0)
    tabT = lambda i: (0, 0, i % nst)
    in_specs = [
        pl.BlockSpec((tm, D), tok),
        pl.BlockSpec((1, D), const),
        pl.BlockSpec(w_nat.shape, const),
        pl.BlockSpec(w_t.shape, const),
        pl.BlockSpec((MLA_Q_LORA, 1), const),
        pl.BlockSpec((1, MLA_KV_LORA), const),
        pl.BlockSpec((MLA_QK_COLS, MLA_Q_LORA), const),
        pl.BlockSpec((3, tm, LANES), tab),
        pl.BlockSpec((3, tm, LANES), tab),
        pl.BlockSpec((3, LANES, tm), tabT),
        pl.BlockSpec((3, LANES, tm), tabT),
    ]
    out_specs = [
        pl.BlockSpec((1, 1, DA_COLS, tm), feat_t),
        pl.BlockSpec((1, tm, DA_HEADS, DA_V_DIM), feat_t),
        pl.BlockSpec((1, tm, DA_HEADS, DA_V_DIM), feat_t),
        pl.BlockSpec((tm, DA_COLS), tok),
        pl.BlockSpec((tm, DA_COLS), tok),
        pl.BlockSpec((tm, MLA_KV_LORA), tok),
        pl.BlockSpec((tm, LANES), tok),
        pl.BlockSpec((tm, MLA_ROPE), tok),
        pl.BlockSpec((1, 1, MLA_QK_COLS, tm), feat_t),
    ]
    out_shape = [
        jax.ShapeDtypeStruct((B, nst, DA_COLS, tm), BF16),
        jax.ShapeDtypeStruct((B, S, DA_HEADS, DA_V_DIM), F32),
        jax.ShapeDtypeStruct((B, S, DA_HEADS, DA_V_DIM), F32),
        jax.ShapeDtypeStruct((n, DA_COLS), BF16),
        jax.ShapeDtypeStruct((n, DA_COLS), BF16),
        jax.ShapeDtypeStruct((n, MLA_KV_LORA), F32),
        jax.ShapeDtypeStruct((n, LANES), F32),
        jax.ShapeDtypeStruct((n, MLA_ROPE), F32),
        jax.ShapeDtypeStruct((B, nst, MLA_QK_COLS, tm), BF16),
    ]
    args = [x.reshape(n, D), norm_g, w_nat, w_t, qg_col, kvg, wuqT, tda, tml,
            tda.transpose(0, 2, 1), tml.transpose(0, 2, 1)]
    if expand:
        assert _n_features(tm) <= MLA_FEAT
        in_specs += [pl.BlockSpec(wuk.shape, const), pl.BlockSpec(wuv.shape, const)]
        out_specs += [pl.BlockSpec((tm, MLA_QK_COLS), tok),
                      pl.BlockSpec((1, 1, MLA_HEADS * MLA_VT_ROWS, tm), feat_t)]
        out_shape += [jax.ShapeDtypeStruct((n, MLA_QK_COLS), BF16),
                      jax.ShapeDtypeStruct((B, nst, MLA_HEADS * MLA_VT_ROWS, tm), BF16)]
        args += [wuk, wuv]
    return pl.pallas_call(
        functools.partial(_inproj_kernel, q_scale=DA_QK_DIM ** -0.5 * LOG2E,
                          qm_scale=MLA_QK_DIM ** -0.5 * LOG2E, expand=expand),
        grid=(n // tm,),
        in_specs=in_specs,
        out_specs=out_specs,
        out_shape=out_shape,
        compiler_params=pltpu.CompilerParams(
            dimension_semantics=("arbitrary",), vmem_limit_bytes=VMEM_LIMIT_BYTES),
        name="inproj",
    )(*args)


def _n_features(tk):
    return tk // CHUNK + 1


def _key_features(tk, valid_rows, lane0, dtype):
    nf = _n_features(tk)
    row = lax.broadcasted_iota(jnp.int32, (tk, LANES), 0)
    f = lax.broadcasted_iota(jnp.int32, (tk, LANES), 1) - lane0
    chunk_hit = (f >= 0) & (f < nf - 1) & (lax.shift_right_logical(row, 6) == f)
    pad_hit = (f == nf - 1) & (row >= valid_rows)
    return jnp.where(chunk_hit | pad_hit, 1.0, 0.0).astype(dtype)


def _query_coeffs(nrows, ncols, tk, tq, q_rel, diag):
    nf = _n_features(tk)
    r = lax.broadcasted_iota(jnp.int32, (nrows, ncols), 0)
    col = lax.broadcasted_iota(jnp.int32, (nrows, ncols), 1)
    q_chunk = lax.shift_right_logical(q_rel + jnp.where(col >= tq, col - tq, col), 6)
    hidden = (r == nf - 1)
    if diag:
        hidden = hidden | ((r < nf - 1) & (r > q_chunk))
    return jnp.where(hidden, NEG_BIG, 0.0).astype(BF16)


def _expand_tile(lat, kr, wuk_ref, wuv_ref, km_ref, vT_ref, *, valid_rows):
    tk = lat.shape[0]
    c = lat.astype(BF16)
    kn = jnp.dot(c, wuk_ref[...], preferred_element_type=F32)
    tail = kr + _key_features(tk, valid_rows, MLA_QK_DIM, F32)
    for i in range(MLA_HEADS):
        lo, hi = i * LANES, (i + 1) * LANES
        km_ref[:, lo:hi] = (kn[:, lo:hi] + tail).astype(BF16)
    vm = jnp.dot(c, wuv_ref[...], preferred_element_type=F32)
    ones = _ones_rows(tk)
    for i in range(MLA_WIDTH // LANES):
        vt = vm[:, i * LANES:(i + 1) * LANES].T.astype(BF16)
        for hh in range(LANES // MLA_V_DIM):
            r0 = (i * (LANES // MLA_V_DIM) + hh) * MLA_VT_ROWS
            vT_ref[0, 0, r0:r0 + MLA_V_DIM, :] = vt[hh * MLA_V_DIM:(hh + 1) * MLA_V_DIM]
            vT_ref[0, 0, r0 + MLA_V_DIM:r0 + MLA_VT_ROWS, :] = ones


def _expand_kernel(c_ref, kr_ref, wuk_ref, wuv_ref, km_ref, vT_ref, *, valid_rows):
    _expand_tile(c_ref[0], kr_ref[0], wuk_ref, wuv_ref, km_ref.at[0], vT_ref, valid_rows=valid_rows)


def _expand(c, kr, wuk, wuv, *, tk, kv_len):
    B, T, _ = c.shape
    nkv = T // tk
    assert _n_features(tk) <= MLA_FEAT
    const = lambda b, j: (0, 0)
    tile = lambda b, j: (b, j, 0)
    return pl.pallas_call(
        functools.partial(_expand_kernel, valid_rows=kv_len - (nkv - 1) * tk),
        grid=(B, nkv),
        in_specs=[
            pl.BlockSpec((1, tk, MLA_KV_LORA), tile),
            pl.BlockSpec((1, tk, LANES), tile),
            pl.BlockSpec((MLA_KV_LORA, MLA_QK_COLS), const),
            pl.BlockSpec((MLA_KV_LORA, MLA_WIDTH), const),
        ],
        out_specs=[
            pl.BlockSpec((1, tk, MLA_QK_COLS), tile),
            pl.BlockSpec((1, 1, MLA_HEADS * MLA_VT_ROWS, tk), lambda b, j: (b, j, 0, 0)),
        ],
        out_shape=[
            jax.ShapeDtypeStruct((B, T, MLA_QK_COLS), BF16),
            jax.ShapeDtypeStruct((B, nkv, MLA_HEADS * MLA_VT_ROWS, tk), BF16),
        ],
        compiler_params=pltpu.CompilerParams(
            dimension_semantics=("arbitrary", "arbitrary"), vmem_limit_bytes=VMEM_LIMIT_BYTES),
        name="kv_expand",
    )(c, kr, wuk, wuv)


def _check_tiling(*, nq, tq, tk, nkv, q_off, kv_len):
    single = nq == 1 and nkv == 1
    causal = q_off == 0 and tq == tk and nq == nkv and kv_len == nkv * tk and (nq * (nq + 1) // 2) % 2 == 0
    assert single or causal, (nq, tq, tk, nkv, q_off, kv_len)


class _Chain(NamedTuple):
    load_k: Callable
    load_vT: Callable
    load_q: Callable
    s: object
    mx: object
    p: object
    a: object
    m: object
    acc: object


def _flash_flat(chains, nq, finalize):
    for c in chains:
        tk = c.s.shape[1]
        c.m[...] = jnp.full(c.m.shape, NEG_BIG, F32)
        c.acc[...] = jnp.zeros(c.acc.shape, F32)

    def advance(u):
        qi, j = u
        last = j == qi
        return jnp.where(last, qi + 1, qi), jnp.where(last, 0, j + 1)

    def scores(u, slot):
        qi, j = u
        for c in chains:
            s = jnp.dot(c.load_k(j), c.load_q(qi, j == qi), preferred_element_type=F32)
            c.s[slot] = s
            c.mx[slot] = jnp.max(s, axis=0, keepdims=True)

    def softmax(u, slot):
        qi, j = u
        for c in chains:
            m_prev = jnp.where(j == 0, NEG_BIG, c.m[...])
            m_new = jnp.maximum(m_prev, c.mx[slot])
            c.m[...] = m_new
            c.a[slot] = jnp.exp2(m_prev - m_new)
            for r in range(0, tk, BF16_ROWS):
                x = (c.s[slot, r:r + BF16_ROWS, :] - m_new).astype(BF16)
                c.p[slot, r:r + BF16_ROWS, :] = jnp.exp2(x)

    def values(u, slot):
        qi, j = u
        par = lax.rem(qi, 2)
        for c in chains:
            for vT, cs in c.load_vT(j):
                pv = jnp.dot(vT, c.p[slot, :, cs], preferred_element_type=F32)
                c.acc[par, :, cs] = c.a[slot, :, cs] * c.acc[par, :, cs] + pv

    def stages(ua, count):
        units = [ua]
        for _ in range(count + 1):
            units.append(advance(units[-1]))
        for i in range(count):
            cur, nxt = (i + 1) % 2, i % 2
            scores(units[i + 2], nxt)
            softmax(units[i + 1], cur)
            values(units[i], nxt)
        for i in range(0, count, 2):
            done_a = units[i][1] == units[i][0]
            done_b = units[i + 1][1] == units[i + 1][0]

            @pl.when(jnp.logical_or(done_a, done_b))
            def _():
                finalize(jnp.where(done_a, units[i][0], units[i + 1][0]))

        return units[count]

    zero = jnp.int32(0)
    u0 = (zero, zero)
    n_units = nq * (nq + 1) // 2
    if n_units == 1:
        scores(u0, 0)
        softmax(u0, 0)
        values(u0, 0)
        finalize(0)
        return

    scores(u0, 0)
    scores(advance(u0), 1)
    softmax(u0, 0)

    ua = lax.fori_loop(0, (n_units - 2) // LOOP_STAGES, lambda t, ua: stages(ua, LOOP_STAGES), u0)
    rest = (n_units - 2) % LOOP_STAGES
    if rest:
        ua = stages(ua, rest)
    ub = advance(ua)
    softmax(ub, 1)
    values(ua, 0)
    values(ub, 1)
    finalize(nq - 1)


def _ones_rows(tk):
    row = lax.broadcasted_iota(jnp.int32, (SUM_PAD, tk), 0)
    return jnp.where(row == 0, 1.0, 0.0).astype(BF16)


def _da_attn_kernel(lam_ref, hg_ref, qT_ref, k_ref, v_ref, o_ref,
                    vT_ref, qs_ref, feat_ref, gv_ref, s_ref, mx_ref, p_ref, a_ref, m_ref, acc_ref,
                    *, nq, tq, tk, nkv, q_off, kv_len, lam_init):
    valid_rows = kv_len - (nkv - 1) * tk
    feat_ref[...] = _key_features(tk, valid_rows, 0, BF16)
    q_rel = q_off if nkv == 1 else 0
    gv_ref[0] = _query_coeffs(LANES, 2 * tq, tk, tq, q_rel, diag=False)
    gv_ref[1] = _query_coeffs(LANES, 2 * tq, tk, tq, q_rel, diag=True)

    row = lax.broadcasted_iota(jnp.int32, (LANES, tq), 0)
    zero = jnp.zeros((LANES, tq), BF16)
    chains = []
    for hh in range(HEAD_PAIR):
        cols = slice(hh * LANES, (hh + 1) * LANES)
        for c in range(nkv):
            vT_ref[hh, c, :DA_V_DIM] = v_ref[0, c * tk:(c + 1) * tk, cols].astype(F32).T.astype(BF16)
            vT_ref[hh, c, DA_V_DIM:] = _ones_rows(tk)
        chains.append(_Chain(
            load_k=lambda j, cols=cols: jnp.concatenate(
                [k_ref[0, pl.ds(pl.multiple_of(j * tk, tk), tk), cols], feat_ref[...]], axis=1),
            load_vT=lambda j, hh=hh: [(vT_ref[hh, j], slice(None))],
            load_q=lambda qi, diag, cols=cols: jnp.concatenate(
                [jnp.concatenate([qT_ref[0, qi, cols.start:cols.start + DA_QK_DIM, :],
                                  jnp.zeros((DA_QK_DIM, tq), BF16)], axis=1),
                 jnp.concatenate([jnp.zeros((DA_QK_DIM, tq), BF16),
                                  qT_ref[0, qi, cols.start + DA_QK_DIM:cols.stop, :]], axis=1),
                 gv_ref[diag.astype(jnp.int32)]], axis=0),
            s=s_ref.at[hh], mx=mx_ref.at[hh], p=p_ref.at[hh], a=a_ref.at[hh], m=m_ref.at[hh],
            acc=acc_ref.at[hh]))

    lf = lam_ref[...]
    lam = (jnp.exp(jnp.sum(lf[0:1] * lf[1:2], axis=-1, keepdims=True))
           - jnp.exp(jnp.sum(lf[2:3] * lf[3:4], axis=-1, keepdims=True)) + lam_init)

    def finalize(qi):
        par = qi % 2
        for hh in range(HEAD_PAIR):
            acc = acc_ref[hh, par]
            o = acc[:DA_V_DIM] * (1.0 / acc[DA_V_DIM:DA_V_DIM + 1])
            o = o[:, :tq] - lam * o[:, tq:]
            y = o * lax.rsqrt(jnp.mean(o * o, axis=0, keepdims=True) + RMS_EPS) * hg_ref[...] * (1.0 - lam_init)
            o_ref[0, qi, hh * LANES:(hh + 1) * LANES, :] = y.astype(BF16)

    _flash_flat(chains, nq, finalize)


def _da_attention(lam, hg, qT, k, v, *, tk, q_off, kv_len, lam_init):
    B, nq, _, tq = qT.shape
    T = k.shape[1]
    nkv = T // tk
    _check_tiling(nq=nq, tq=tq, tk=tk, nkv=nkv, q_off=q_off, kv_len=kv_len)
    assert _n_features(tk) <= LANES
    pair = HEAD_PAIR * LANES
    head_kv = lambda b, h: (b, 0, h)
    return pl.pallas_call(
        functools.partial(_da_attn_kernel, nq=nq, tq=tq, tk=tk, nkv=nkv, q_off=q_off, kv_len=kv_len,
                          lam_init=lam_init),
        grid=(B, DA_HEADS // HEAD_PAIR),
        in_specs=[
            pl.BlockSpec(lam.shape, lambda b, h: (0, 0)),
            pl.BlockSpec((DA_V_DIM, 1), lambda b, h: (0, 0)),
            pl.BlockSpec((1, nq, pair, tq), lambda b, h: (b, 0, h, 0)),
            pl.BlockSpec((1, T, pair), head_kv),
            pl.BlockSpec((1, T, pair), head_kv),
        ],
        out_specs=pl.BlockSpec((1, nq, pair, tq), lambda b, h: (b, 0, h, 0)),
        out_shape=jax.ShapeDtypeStruct((B, nq, DA_COLS, tq), BF16),
        scratch_shapes=[
            pltpu.VMEM((HEAD_PAIR, nkv, DA_V_DIM + SUM_PAD, tk), BF16),
            pltpu.VMEM((HEAD_PAIR, nq, LANES, 2 * tq), BF16),
            pltpu.VMEM((tk, LANES), BF16),
            pltpu.VMEM((2, LANES, 2 * tq), BF16),
            pltpu.VMEM((HEAD_PAIR, 2, tk, 2 * tq), F32),
            pltpu.VMEM((HEAD_PAIR, 2, 1, 2 * tq), F32),
            pltpu.VMEM((HEAD_PAIR, 2, tk, 2 * tq), BF16),
            pltpu.VMEM((HEAD_PAIR, 2, 1, 2 * tq), F32),
            pltpu.VMEM((HEAD_PAIR, 1, 2 * tq), F32),
            pltpu.VMEM((HEAD_PAIR, 2, DA_V_DIM + SUM_PAD, 2 * tq), F32),
        ],
        compiler_params=pltpu.CompilerParams(
            dimension_semantics=("arbitrary", "arbitrary"), vmem_limit_bytes=VMEM_LIMIT_BYTES),
        name="da_attention",
    )(lam, hg, qT, k, v)


def _mla_attn_kernel(qT_ref, k_ref, vT_ref, o_ref, gv_ref, s_ref, mx_ref, p_ref, a_ref, m_ref, acc_ref,
                     *, nq, tq, tk, nkv, q_off):
    q_rel = q_off if nkv == 1 else 0
    gv_ref[0] = _query_coeffs(MLA_FEAT, tq, tk, tq, q_rel, diag=False)
    gv_ref[1] = _query_coeffs(MLA_FEAT, tq, tk, tq, q_rel, diag=True)
    zero = jnp.zeros((LANES, tq), BF16)
    halves = (slice(0, tq), slice(tq, 2 * tq))
    chains = []
    for cc in range(MLA_GROUP):
        h0 = 2 * cc

        def load_q(qi, diag, h0=h0):
            gv = gv_ref[diag.astype(jnp.int32)]
            qa, qb = (jnp.concatenate([qT_ref[0, qi, h * LANES:h * LANES + MLA_QK_DIM, :], gv], axis=0)
                      for h in (h0, h0 + 1))
            return jnp.concatenate([jnp.concatenate([qa, zero], axis=1),
                                    jnp.concatenate([zero, qb], axis=1)], axis=0)

        chains.append(_Chain(
            load_k=lambda j, h0=h0: k_ref[0, pl.ds(pl.multiple_of(j * tk, tk), tk),
                                          h0 * LANES:(h0 + 2) * LANES],
            load_vT=lambda j, h0=h0: [
                (vT_ref[0, j, (h0 + i) * MLA_VT_ROWS:(h0 + i + 1) * MLA_VT_ROWS, :], halves[i]) for i in range(2)],
            load_q=load_q,
            s=s_ref.at[cc], mx=mx_ref.at[cc], p=p_ref.at[cc], a=a_ref.at[cc], m=m_ref.at[cc],
            acc=acc_ref.at[cc]))

    def finalize(qi):
        par = qi % 2
        for cc in range(MLA_GROUP):
            acc = acc_ref[cc, par]
            o = acc[:MLA_V_DIM] * (1.0 / acc[MLA_V_DIM:MLA_V_DIM + 1])
            o_ref[0, qi, cc * LANES:(cc + 1) * LANES, :] = jnp.concatenate(
                [o[:, halves[0]], o[:, halves[1]]], axis=0).astype(BF16)

    _flash_flat(chains, nq, finalize)


def _mla_attention(qmT, km, vT, *, q_off, kv_len):
    B, nq, _, tq = qmT.shape
    nkv, tk = vT.shape[1], vT.shape[3]
    T = km.shape[1]
    _check_tiling(nq=nq, tq=tq, tk=tk, nkv=nkv, q_off=q_off, kv_len=kv_len)
    heads = 2 * MLA_GROUP
    ncols = 2 * tq
    return pl.pallas_call(
        functools.partial(_mla_attn_kernel, nq=nq, tq=tq, tk=tk, nkv=nkv, q_off=q_off),
        grid=(B, MLA_HEADS // heads),
        in_specs=[
            pl.BlockSpec((1, nq, heads * LANES, tq), lambda b, h: (b, 0, h, 0)),
            pl.BlockSpec((1, T, heads * LANES), lambda b, h: (b, 0, h)),
            pl.BlockSpec((1, nkv, heads * MLA_VT_ROWS, tk), lambda b, h: (b, 0, h, 0)),
        ],
        out_specs=pl.BlockSpec((1, nq, heads * MLA_V_DIM, tq), lambda b, h: (b, 0, h, 0)),
        out_shape=jax.ShapeDtypeStruct((B, nq, MLA_WIDTH, tq), BF16),
        scratch_shapes=[
            pltpu.VMEM((2, MLA_FEAT, tq), BF16),
            pltpu.VMEM((MLA_GROUP, 2, tk, ncols), F32),
            pltpu.VMEM((MLA_GROUP, 2, 1, ncols), F32),
            pltpu.VMEM((MLA_GROUP, 2, tk, ncols), BF16),
            pltpu.VMEM((MLA_GROUP, 2, 1, ncols), F32),
            pltpu.VMEM((MLA_GROUP, 1, ncols), F32),
            pltpu.VMEM((MLA_GROUP, 2, MLA_VT_ROWS, ncols), F32),
        ],
        compiler_params=pltpu.CompilerParams(
            dimension_semantics=("arbitrary", "arbitrary"), vmem_limit_bytes=VMEM_LIMIT_BYTES),
        name="mla_attention",
    )(qmT, km, vT)


def _epilogue_kernel(x_ref, oa_ref, ob_ref, ng_ref, wzg_ref, gb_ref, wa_ref, wb_ref, wo_ref, fg_ref,
                     y_ref, *, final_norm):
    x = x_ref[...]
    d = x.shape[1]
    h = _rms(x, ng_ref[...]).astype(BF16)
    zg = jnp.dot(h, wzg_ref[...], preferred_element_type=F32)
    untranspose = lambda ref: jnp.concatenate(
        [ref[0, 0, c:c + LANES, :].astype(F32).T for c in range(0, ref.shape[2], LANES)], axis=1)
    ga = (untranspose(oa_ref) * jax.nn.silu(zg[:, :DA_COLS])).astype(BF16)
    gb = (untranspose(ob_ref) * jax.nn.silu(zg[:, DA_COLS:DA_COLS + MLA_WIDTH])).astype(BF16)
    ya = jnp.dot(ga, wa_ref[...], preferred_element_type=F32)
    yb = jnp.dot(gb, wb_ref[...], preferred_element_type=F32)
    g = jax.nn.sigmoid(zg[:, DA_COLS + MLA_WIDTH:] + gb_ref[...])
    m = (g[:, :d] * ya + g[:, d:] * yb).astype(BF16)
    out = x + jnp.dot(m, wo_ref[...], preferred_element_type=F32)
    y_ref[...] = _rms(out, fg_ref[...]) if final_norm else out


def _epilogue(x, oaT, obT, norm_g, wzg, gate_b, wa, wb, wo, final_g, *, final_norm):
    n, D = x.shape
    _, nst, _, tm = oaT.shape
    const = lambda i: (0, 0)
    tok = lambda i: (i, 0)
    tile_t = lambda i: (i // nst, i % nst, 0, 0)
    return pl.pallas_call(
        functools.partial(_epilogue_kernel, final_norm=final_norm),
        grid=(n // tm,),
        in_specs=[
            pl.BlockSpec((tm, D), tok),
            pl.BlockSpec((1, 1, DA_COLS, tm), tile_t),
            pl.BlockSpec((1, 1, MLA_WIDTH, tm), tile_t),
            pl.BlockSpec((1, D), const),
            pl.BlockSpec(wzg.shape, const),
            pl.BlockSpec((1, 2 * D), const),
            pl.BlockSpec(wa.shape, const),
            pl.BlockSpec(wb.shape, const),
            pl.BlockSpec(wo.shape, const),
            pl.BlockSpec((1, D), const),
        ],
        out_specs=pl.BlockSpec((tm, D), tok),
        out_shape=jax.ShapeDtypeStruct((n, D), F32),
        compiler_params=pltpu.CompilerParams(
            dimension_semantics=("arbitrary",), vmem_limit_bytes=VMEM_LIMIT_BYTES),
        name="epilogue",
    )(x, oaT, obT, norm_g, wzg, gate_b, wa, wb, wo, final_g)


def _tiles(S):
    t = min(512, S)
    assert S % t == 0 and t % LANES == 0
    return t


def _prep_weights(w_in, mla_w_uq, mla_w_uk):
    D = w_in.shape[0]
    sizes = (DA_COLS, DA_COLS, DA_COLS, DA_COLS, MLA_Q_LORA, MLA_KV_LORA, MLA_ROPE, MLA_WIDTH, 2 * D)
    assert w_in.shape[1] == sum(sizes)
    w_q, w_k, w_v, w_za, w_cq, w_ckv, w_kr, w_zb, w_g = jnp.split(w_in, np.cumsum(sizes)[:-1].tolist(), axis=1)
    w_kr = jnp.pad(w_kr, ((0, 0), (MLA_NOPE, LANES - MLA_QK_DIM)))
    w_nat = jnp.concatenate([w_k, w_v, w_ckv, w_kr], axis=1).astype(BF16)
    w_t = jnp.concatenate([w_q, w_cq], axis=1).T.astype(BF16)
    w_zg = jnp.concatenate([w_za, w_zb, w_g], axis=1).astype(BF16)
    pad_heads = lambda w, dh: jnp.pad(
        w.reshape(w.shape[0], MLA_HEADS, dh), ((0, 0), (0, 0), (0, LANES - dh))
    ).reshape(w.shape[0], MLA_QK_COLS).astype(BF16)
    return w_nat, w_t, w_zg, pad_heads(mla_w_uq, MLA_QK_DIM).T, pad_heads(mla_w_uk, MLA_NOPE)


class _LayerWeights(NamedTuple):
    norm_g: jax.Array
    w_nat: jax.Array
    w_t: jax.Array
    w_zg: jax.Array
    gate_b: jax.Array
    da_lambda: jax.Array
    da_head_norm_g: jax.Array
    mla_q_norm_g: jax.Array
    mla_kv_norm_g: jax.Array
    wuqT: jax.Array
    wuk: jax.Array
    wuv: jax.Array
    wa: jax.Array
    wb: jax.Array
    wo: jax.Array
    final_norm_g: jax.Array


def _layer(x, past, lam_init, final_norm, w):
    B, S, D = x.shape
    past_len = 0 if past is None else past[0].shape[1]
    Sp = -(-S // LANES) * LANES
    xp = jnp.pad(x, ((0, 0), (0, Sp - S), (0, 0))) if Sp != S else x
    tm = _tiles(Sp)
    pos = past_len + jnp.arange(Sp, dtype=jnp.int32)
    tda = _rope_tables(pos, DA_ROT, _da_lane)
    tml = _rope_tables(pos, MLA_ROPE, _mla_lane)

    outs = _inproj(xp, w.norm_g, w.w_nat, w.w_t, w.mla_q_norm_g, w.mla_kv_norm_g, w.wuqT, tda, tml, w.wuk, w.wuv,
                   tm=tm, expand=past is None)
    qT, k_new, v_new, k_b, v_b, lat_new, kr_new, kr32, qmT = outs[:9]
    lat_new = lat_new.reshape(B, Sp, MLA_KV_LORA)

    kv_len = past_len + S
    if past is None:
        k_all, v_all = k_b.reshape(B, Sp, DA_COLS), v_b.reshape(B, Sp, DA_COLS)
        km, vmT = outs[9].reshape(B, Sp, MLA_QK_COLS), outs[10]
        tk = tm
    else:
        pk, pv, pc, pr = past
        tk = -(-kv_len // LANES) * LANES
        cat = lambda old, new: jnp.pad(jnp.concatenate([old, new[:, :S]], axis=1),
                                       ((0, 0), (0, tk - kv_len), (0, 0)))
        k_all = cat(pk.reshape(B, past_len, DA_COLS), k_new.reshape(B, Sp, DA_COLS)).astype(BF16)
        v_all = cat(pv.reshape(B, past_len, DA_COLS), v_new.reshape(B, Sp, DA_COLS)).astype(BF16)
        c_all = cat(pc, lat_new)
        kr_all = cat(jnp.pad(pr, ((0, 0), (0, 0), (MLA_NOPE, LANES - MLA_QK_DIM))), kr_new.reshape(B, Sp, LANES))
        km, vmT = _expand(c_all, kr_all, w.wuk, w.wuv, tk=tk, kv_len=kv_len)

    o_a = _da_attention(w.da_lambda, w.da_head_norm_g, qT, k_all, v_all,
                        tk=tk, q_off=past_len, kv_len=kv_len, lam_init=lam_init)
    o_b = _mla_attention(qmT, km, vmT, q_off=past_len, kv_len=kv_len)

    n = B * Sp
    y = _epilogue(xp.reshape(n, D), o_a, o_b, w.norm_g, w.w_zg,
                  w.gate_b, w.wa, w.wb, w.wo, w.final_norm_g, final_norm=final_norm)
    y = y.reshape(B, Sp, D)[:, :S]
    new = (k_new[:, :S], v_new[:, :S], lat_new[:, :S], kr32.reshape(B, Sp, MLA_ROPE)[:, :S])
    return y, new


def kernel(x_prompt, x_sample, cache_da_k, cache_da_v, cache_mla_latent, cache_mla_krope, norm_g, w_in, gate_b, da_lambda, da_head_norm_g, mla_q_norm_g, mla_kv_norm_g, mla_w_uq, mla_w_uk, mla_w_uv, w_branch_a, w_branch_b, w_out, final_norm_g):
    depth = w_in.shape[0]
    at = lambda a, l: a.reshape(a.shape[1:]) if depth == 1 else a[l]
    row = lambda v: v.reshape(1, -1)
    hp, hs = x_prompt, x_sample
    rows_p, rows_s = [], []
    for l in range(depth):
        lam_init = 0.8 - 0.6 * math.exp(-0.3 * l)
        last = l == depth - 1
        w_nat, w_t, w_zg, wuqT, wuk = _prep_weights(at(w_in, l), at(mla_w_uq, l), at(mla_w_uk, l))
        w = _LayerWeights(
            norm_g=row(at(norm_g, l)), w_nat=w_nat, w_t=w_t, w_zg=w_zg, gate_b=row(at(gate_b, l)),
            da_lambda=at(da_lambda, l), da_head_norm_g=at(da_head_norm_g, l).reshape(-1, 1),
            mla_q_norm_g=at(mla_q_norm_g, l).reshape(-1, 1), mla_kv_norm_g=row(at(mla_kv_norm_g, l)),
            wuqT=wuqT, wuk=wuk, wuv=at(mla_w_uv, l).astype(BF16), wa=at(w_branch_a, l).astype(BF16),
            wb=at(w_branch_b, l).astype(BF16), wo=at(w_out, l).astype(BF16), final_norm_g=row(final_norm_g))
        hp, new_p = _layer(hp, None, lam_init, last, w)
        past = (at(cache_da_k, l), at(cache_da_v, l), at(cache_mla_latent, l), at(cache_mla_krope, l))
        hs, new_s = _layer(hs, past, lam_init, last, w)
        rows_p.append(new_p)
        rows_s.append(new_s)
    stack = lambda rows, i: jnp.stack([r[i] for r in rows], 0)
    return (hp, hs, stack(rows_p, 0), stack(rows_p, 1), stack(rows_p, 2), stack(rows_p, 3),
            stack(rows_s, 0), stack(rows_s, 1), stack(rows_s, 2), stack(rows_s, 3))
```
